```python
import jax, jax.numpy as jnp
from jax import lax
import numpy as np

D_MODEL = 1024
BATCH = 1
SEQ = 16384
DEPTH = 2
DEC_BATCH = 16
DEC_SEQ = 2048
PAST_LEN = 128

HEAD_DIM = 64
ROPE_THETA = 500000.0
NORM_EPS = 1e-6
N_EVEN = (DEPTH + 1) // 2
N_ODD = DEPTH // 2
MLA_HEADS = 8
MLA_NOPE = 64
MLA_ROPE = 32
MLA_V = 64
MLA_Q_RANK = 256
MLA_KV_RANK = 128
Q_BLOCK = 128
DIL_PATTERNS = ((128, 1), (512, 4), (2048, 16))
DIL_GROUPS = len(DIL_PATTERNS)
DIL_HEADS = 8
DIL_ROT = HEAD_DIM // 4
MLA_IN = MLA_Q_RANK + MLA_KV_RANK + MLA_ROPE
DIL_QKV = 3 * DIL_GROUPS * DIL_HEADS * HEAD_DIM
MIX_IN = MLA_IN + DIL_QKV
MIX_OUT = MLA_HEADS * MLA_V + DIL_HEADS * HEAD_DIM
D_RNN = 1536
LRU_BLOCKS = 12
LRU_BW = D_RNN // LRU_BLOCKS
CONV_W = 4
CONV_LEFT = 2
LRU_C = 8.0
D_FF = ((8 * D_MODEL // 3 + 255) // 256) * 256
NEG_BIG = -1e30

kernel_name = "mla_dilated_rglru_encoder"


def rmsnorm(x, g):
    xf = x.astype(jnp.float32)
    y = xf * lax.rsqrt(jnp.mean(xf * xf, axis=-1, keepdims=True) + NORM_EPS)
    return (y * g.astype(jnp.float32)).astype(x.dtype)


def rope(x, rot_dim):
    S = x.shape[1]
    half = rot_dim // 2
    inv = jnp.power(ROPE_THETA, -jnp.arange(half, dtype=jnp.float32) / half)
    ang = jnp.arange(S, dtype=jnp.float32)[:, None] * inv[None, :]
    shape = (1, S) + (1,) * (x.ndim - 3) + (half,)
    cos = jnp.cos(ang).reshape(shape)
    sin = jnp.sin(ang).reshape(shape)
    xr = x[..., :rot_dim].astype(jnp.float32)
    x1, x2 = xr[..., :half], xr[..., half:]
    rot = jnp.concatenate([x1 * cos - x2 * sin, x2 * cos + x1 * sin], axis=-1).astype(x.dtype)
    return jnp.concatenate([rot, x[..., rot_dim:]], axis=-1)


def mla_attention(q, k, v):
    B, S, H, Dq = q.shape
    nb = S // Q_BLOCK
    scale = Dq ** -0.5
    qb = q.reshape(B, nb, Q_BLOCK, H, Dq).transpose(1, 0, 2, 3, 4)

    def block(qblk):
        s = jnp.einsum('bqhd,bkhd->bhqk', qblk, k, preferred_element_type=jnp.float32) * scale
        p = jax.nn.softmax(s, axis=-1)
        return jnp.einsum('bhqk,bkhd->bqhd', p.astype(v.dtype), v)

    o = lax.map(block, qb)
    return o.transpose(1, 0, 2, 3, 4).reshape(B, S, H, v.shape[-1])


def dilated_group(q, k, v, dilation, steps):
    B, S, H, Dh = q.shape
    unit = dilation * steps
    Sp = -(-S // unit) * unit
    pad = Sp - S
    L = Sp // dilation
    nb = L // steps

    def split(t):
        t = jnp.pad(t, ((0, 0), (0, pad), (0, 0), (0, 0)))
        t = t.reshape(B, L, dilation, H, Dh).transpose(0, 2, 1, 3, 4)
        return t.reshape(B, dilation, nb, steps, H, Dh)

    def neighbours(t):
        z = jnp.zeros_like(t[:, :, :1])
        prev = jnp.concatenate([z, t[:, :, :-1]], axis=2)
        nxt = jnp.concatenate([t[:, :, 1:], z], axis=2)
        return jnp.concatenate([prev, t, nxt], axis=3)

    qs = split(q)
    kn = neighbours(split(k))
    vn = neighbours(split(v))
    valid = (jnp.arange(Sp) < S).reshape(L, dilation).T.reshape(dilation, nb, steps)
    vmask = neighbours(valid[None])[0]
    rel = jnp.arange(3 * steps)[None, :] - steps - jnp.arange(steps)[:, None]
    band = jnp.abs(rel) <= steps
    mask = band[None, None] & vmask[:, :, None, :]

    s = jnp.einsum('bdgqhe,bdgkhe->bdghqk', qs, kn, preferred_element_type=jnp.float32) * (Dh ** -0.5)
    s = jnp.where(mask[None, :, :, None], s, NEG_BIG)
    m = jnp.max(s, axis=-1, keepdims=True)
    e = jnp.exp(s - m)
    den = jnp.sum(e, axis=-1, keepdims=True)
    o = jnp.einsum('bdghqk,bdgkhe->bdgqhe', e / den, vn.astype(jnp.float32))
    lse = (m + jnp.log(den))[..., 0]

    o = o.reshape(B, dilation, L, H, Dh).transpose(0, 2, 1, 3, 4).reshape(B, Sp, H, Dh)[:, :S]
    lse = lse.transpose(0, 1, 2, 4, 3).reshape(B, dilation, L, H).transpose(0, 2, 1, 3)
    lse = lse.reshape(B, Sp, H)[:, :S]
    return o, lse


def attention_mixer(h, w_in, q_norm, w_uq, kv_norm, w_ukv, w_out):
    B, S, _ = h.shape
    z = h @ w_in
    c_q = z[..., :MLA_Q_RANK]
    c_kv = z[..., MLA_Q_RANK:MLA_Q_RANK + MLA_KV_RANK]
    k_r = z[..., MLA_Q_RANK + MLA_KV_RANK:MLA_IN]
    qkv = z[..., MLA_IN:]

    q = (rmsnorm(c_q, q_norm) @ w_uq).reshape(B, S, MLA_HEADS, MLA_NOPE + MLA_ROPE)
    q = jnp.concatenate([q[..., :MLA_NOPE], rope(q[..., MLA_NOPE:], MLA_ROPE)], axis=-1)
    kv = (rmsnorm(c_kv, kv_norm) @ w_ukv).reshape(B, S, MLA_HEADS, MLA_NOPE + MLA_V)
    k_nope, v = kv[..., :MLA_NOPE], kv[..., MLA_NOPE:]
    k_r = rope(k_r, MLA_ROPE)
    k = jnp.concatenate([k_nope, jnp.broadcast_to(k_r[:, :, None], (B, S, MLA_HEADS, MLA_ROPE))], axis=-1)
    o_mla = mla_attention(q, k, v).reshape(B, S, MLA_HEADS * MLA_V)

    qkv = qkv.reshape(B, S, 3, DIL_GROUPS, DIL_HEADS, HEAD_DIM)
    dq = rope(qkv[:, :, 0], DIL_ROT)
    dk = rope(qkv[:, :, 1], DIL_ROT)
    dv = qkv[:, :, 2]
    outs, lses = [], []
    for g, (window, dilation) in enumerate(DIL_PATTERNS):
        o_g, l_g = dilated_group(dq[:, :, g], dk[:, :, g], dv[:, :, g], dilation, window // (2 * dilation))
        outs.append(o_g)
        lses.append(l_g)
    wts = jax.nn.softmax(jnp.stack(lses, axis=0), axis=0)[..., None]
    o_dil = jnp.sum(wts * jnp.stack(outs, axis=0), axis=0).reshape(B, S, DIL_HEADS * HEAD_DIM)

    o = jnp.concatenate([o_mla, o_dil.astype(h.dtype)], axis=-1)
    return (o @ w_out).astype(h.dtype)


def rglru_scan(x, w_gate, b_gate, lam, reverse):
    B, S, _ = x.shape
    xb = x.reshape(B, S, LRU_BLOCKS, LRU_BW)
    g = jnp.einsum('bsnc,kncd->kbsnd', xb, w_gate).reshape(2, B, S, D_RNN) + b_gate[:, None, None, :]
    r = jax.nn.sigmoid(g[0].astype(jnp.float32))
    i = jax.nn.sigmoid(g[1].astype(jnp.float32))
    log_a = -LRU_C * r * jax.nn.softplus(-lam.astype(jnp.float32))
    a = jnp.exp(log_a)
    u = jnp.sqrt(-jnp.expm1(2.0 * log_a)) * (i * x.astype(jnp.float32))

    def comb(p, q):
        a1, b1 = p
        a2, b2 = q
        return a1 * a2, a2 * b1 + b2

    _, hs = lax.associative_scan(comb, (a, u), reverse=reverse, axis=1)
    return hs


def recurrent_mixer(h, w_in, conv_w, conv_b, w_gate, b_gate, lam, w_out):
    z = h @ w_in
    y = jax.nn.gelu(z[..., :D_RNN])
    xr = z[..., D_RNN:]
    xc = lax.conv_general_dilated(
        xr, conv_w[:, None, :], window_strides=(1,),
        padding=[(CONV_LEFT, CONV_W - 1 - CONV_LEFT)],
        dimension_numbers=('NWC', 'WIO', 'NWC'), feature_group_count=D_RNN) + conv_b
    hf = rglru_scan(xc, w_gate[0], b_gate[0], lam[0], reverse=False)
    hb = rglru_scan(xc, w_gate[1], b_gate[1], lam[1], reverse=True)
    o = ((hf + hb) * y.astype(jnp.float32)).astype(h.dtype)
    return (o @ w_out).astype(h.dtype)


def swiglu(h, w_gu, w_down):
    gu = h @ w_gu
    return ((jax.nn.silu(gu[..., :D_FF]) * gu[..., D_FF:]) @ w_down).astype(h.dtype)


def trunk(x, norm_mix, w_in_a, q_norm, w_uq, kv_norm, w_ukv, w_out_a,
          w_in_r, conv_w, conv_b, lru_w_gate, lru_b_gate, lru_lambda, w_out_r,
          norm_ffn, w_gu, w_down, norm_final):
    for layer in range(DEPTH):
        h = rmsnorm(x, norm_mix[layer])
        j = layer // 2
        if layer % 2 == 0:
            x = x + attention_mixer(h, w_in_a[j], q_norm[j], w_uq[j], kv_norm[j], w_ukv[j], w_out_a[j])
        else:
            x = x + recurrent_mixer(h, w_in_r[j], conv_w[j], conv_b[j], lru_w_gate[j],
                                    lru_b_gate[j], lru_lambda[j], w_out_r[j])
        x = x + swiglu(rmsnorm(x, norm_ffn[layer]), w_gu[layer], w_down[layer])
    return rmsnorm(x, norm_final)


def setup_inputs(seed: int = 0) -> dict:
    key = jax.random.key(seed)
    ks = jax.random.split(key, 24)
    f32 = jnp.float32

    def nrm(k, shape, fan_in):
        return jax.random.normal(k, shape, f32) * (fan_in ** -0.5)

    def gain(k, shape):
        return 1.0 + 0.01 * jax.random.normal(k, shape, f32)

    u = jax.random.uniform(ks[20], (N_ODD, 2, D_RNN), f32, 0.9, 0.999)
    s = u ** (1.0 / LRU_C)
    lam = jnp.log(s) - jnp.log1p(-s)
    return {
        "x_prompt": jax.random.normal(ks[0], (BATCH, SEQ, D_MODEL), f32),
        "x_sample": jax.random.normal(ks[1], (DEC_BATCH, DEC_SEQ, D_MODEL), f32),
        "norm_mix": gain(ks[2], (DEPTH, D_MODEL)),
        "w_in_a": nrm(ks[3], (N_EVEN, D_MODEL, MIX_IN), D_MODEL),
        "q_norm": gain(ks[4], (N_EVEN, MLA_Q_RANK)),
        "w_uq": nrm(ks[5], (N_EVEN, MLA_Q_RANK, MLA_HEADS * (MLA_NOPE + MLA_ROPE)), MLA_Q_RANK),
        "kv_norm": gain(ks[6], (N_EVEN, MLA_KV_RANK)),
        "w_ukv": nrm(ks[7], (N_EVEN, MLA_KV_RANK, MLA_HEADS * (MLA_NOPE + MLA_V)), MLA_KV_RANK),
        "w_out_a": nrm(ks[8], (N_EVEN, MIX_OUT, D_MODEL), MIX_OUT),
        "w_in_r": nrm(ks[9], (N_ODD, D_MODEL, 2 * D_RNN), D_MODEL),
        "conv_w": nrm(ks[10], (N_ODD, CONV_W, D_RNN), CONV_W),
        "conv_b": 0.01 * jax.random.normal(ks[11], (N_ODD, D_RNN), f32),
        "lru_w_gate": nrm(ks[12], (N_ODD, 2, 2, LRU_BLOCKS, LRU_BW, LRU_BW), LRU_BW),
        "lru_b_gate": 0.01 * jax.random.normal(ks[13], (N_ODD, 2, 2, D_RNN), f32),
        "lru_lambda": lam,
        "w_out_r": nrm(ks[14], (N_ODD, D_RNN, D_MODEL), D_RNN),
        "norm_ffn": gain(ks[15], (DEPTH, D_MODEL)),
        "w_gu": nrm(ks[16], (DEPTH, D_MODEL, 2 * D_FF), D_MODEL),
        "w_down": nrm(ks[17], (DEPTH, D_FF, D_MODEL), D_FF),
        "norm_final": gain(ks[18], (D_MODEL,)),
    }


def reference(x_prompt, x_sample, norm_mix, w_in_a, q_norm, w_uq, kv_norm, w_ukv, w_out_a,
              w_in_r, conv_w, conv_b, lru_w_gate, lru_b_gate, lru_lambda, w_out_r,
              norm_ffn, w_gu, w_down, norm_final):
    y_prompt = trunk(x_prompt, norm_mix, w_in_a, q_norm, w_uq, kv_norm, w_ukv, w_out_a,
                     w_in_r, conv_w, conv_b, lru_w_gate, lru_b_gate, lru_lambda, w_out_r,
                     norm_ffn, w_gu, w_down, norm_final)
    y_sample = trunk(x_sample, norm_mix, w_in_a, q_norm, w_uq, kv_norm, w_ukv, w_out_a,
                     w_in_r, conv_w, conv_b, lru_w_gate, lru_b_gate, lru_lambda, w_out_r,
                     norm_ffn, w_gu, w_down, norm_final)
    return (y_prompt, y_sample)
```

```python
import functools

import numpy as np
import jax
import jax.numpy as jnp
from jax import lax
from jax.experimental import pallas as pl
from jax.experimental.pallas import tpu as pltpu

F32 = jnp.float32
BF16 = jnp.bfloat16

D_MODEL = 1024
HEAD_DIM = 64
ROPE_THETA = 500000.0
NORM_EPS = 1e-6
MLA_HEADS = 8
MLA_NOPE = 64
MLA_ROPE = 32
MLA_V = 64
MLA_Q_RANK = 256
MLA_KV_RANK = 128
DIL_PATTERNS = ((128, 1), (512, 4), (2048, 16))
DIL_GROUPS = len(DIL_PATTERNS)
DIL_HEADS = 8
DIL_ROT = HEAD_DIM // 4
DIL_STEPS = 64
MLA_IN = MLA_Q_RANK + MLA_KV_RANK + MLA_ROPE
DIL_W = DIL_HEADS * HEAD_DIM
DIL_QKV = 3 * DIL_GROUPS * DIL_W
D_RNN = 1536
LRU_BLOCKS = 12
LRU_BW = D_RNN // LRU_BLOCKS
CONV_W = 4
CONV_LEFT = 2
LRU_C = 8.0
D_FF = ((8 * D_MODEL // 3 + 255) // 256) * 256
NEG_BIG = -1e30

LANES = 128
SUBLANES = 8
MLA_PAD = 128
VMEM_LIMIT = 56 * 1024 * 1024

FF_CHUNKS = 2
FF_CHUNK = D_FF // FF_CHUNKS


def _cparams(sem):
    return pltpu.CompilerParams(dimension_semantics=sem, vmem_limit_bytes=VMEM_LIMIT)


def _const_spec(shape):
    nd = len(shape)
    return pl.BlockSpec(shape, lambda *_: (0,) * nd, pipeline_mode=pl.Buffered(1))


def _rms(x, g):
    return x * lax.rsqrt(jnp.mean(x * x, axis=-1, keepdims=True) + NORM_EPS) * g


def _rope_block(xb, c, s1, s2, half):
    return xb * c + pltpu.roll(xb, LANES - half, 1) * s1 + pltpu.roll(xb, half, 1) * s2


def _l0_prep_kernel(x_ref, g_ref, wmla_ref, wdil_ref, qn_ref, wuq_ref, kvn_ref, wuk_ref, wuv_ref,
                    tab_ref, q_out, k_out, v_out, d_out):
    x = x_ref[0]
    h = _rms(x, g_ref[...]).astype(BF16)
    z = jnp.dot(h, wmla_ref[...], preferred_element_type=F32)
    qn = _rms(z[:, :MLA_Q_RANK], qn_ref[...]).astype(BF16)
    kvn = _rms(z[:, MLA_Q_RANK:MLA_Q_RANK + MLA_KV_RANK], kvn_ref[...]).astype(BF16)
    kr = z[:, MLA_Q_RANK + MLA_KV_RANK:]
    qf = jnp.dot(qn, wuq_ref[...], preferred_element_type=F32)
    kf = jnp.dot(kvn, wuk_ref[...], preferred_element_type=F32)
    vf = jnp.dot(kvn, wuv_ref[...], preferred_element_type=F32)
    half = MLA_ROPE // 2
    krr = _rope_block(kr, tab_ref[3], tab_ref[4], tab_ref[5], half)
    for hh in range(MLA_HEADS):
        sl = slice(hh * MLA_PAD, (hh + 1) * MLA_PAD)
        q_out[0, :, sl] = _rope_block(qf[:, sl], tab_ref[0], tab_ref[1], tab_ref[2], half).astype(BF16)
        k_out[0, :, sl] = (kf[:, sl] + krr).astype(BF16)
    v_out[0] = vf.astype(BF16)
    dhalf = DIL_ROT // 2
    for j in range(3 * DIL_GROUPS):
        zc = jnp.dot(h, wdil_ref[:, j * DIL_W:(j + 1) * DIL_W], preferred_element_type=F32)
        if j < 2 * DIL_GROUPS:
            t0 = 6 if j < DIL_GROUPS else 9
            for b in range(DIL_W // LANES):
                sl = slice(b * LANES, (b + 1) * LANES)
                d_out[0, :, j * DIL_W + b * LANES:j * DIL_W + (b + 1) * LANES] = _rope_block(
                    zc[:, sl], tab_ref[t0], tab_ref[t0 + 1], tab_ref[t0 + 2], dhalf).astype(BF16)
        else:
            d_out[0, :, j * DIL_W:(j + 1) * DIL_W] = zc.astype(BF16)


def _l0_prep(x, g, wp, tabs, tm):
    B, S, D = x.shape
    grid = (B, S // tm)
    row = lambda w: pl.BlockSpec((1, tm, w), lambda b, i: (b, i, 0))
    return pl.pallas_call(
        _l0_prep_kernel,
        grid=grid,
        in_specs=[
            row(D),
            _const_spec((1, D)),
            _const_spec(wp["w_mla"].shape),
            _const_spec(wp["w_dil"].shape),
            _const_spec((1, MLA_Q_RANK)),
            _const_spec(wp["w_uq"].shape),
            _const_spec((1, MLA_KV_RANK)),
            _const_spec(wp["w_uk"].shape),
            _const_spec(wp["w_uv"].shape),
            pl.BlockSpec((12, tm, LANES), lambda b, i: (0, i, 0)),
        ],
        out_specs=[row(MLA_HEADS * MLA_PAD), row(MLA_HEADS * MLA_PAD), row(MLA_HEADS * MLA_V), row(DIL_QKV)],
        out_shape=[
            jax.ShapeDtypeStruct((B, S, MLA_HEADS * MLA_PAD), BF16),
            jax.ShapeDtypeStruct((B, S, MLA_HEADS * MLA_PAD), BF16),
            jax.ShapeDtypeStruct((B, S, MLA_HEADS * MLA_V), BF16),
            jax.ShapeDtypeStruct((B, S, DIL_QKV), BF16),
        ],
        compiler_params=_cparams(("parallel", "parallel")),
    )(x, g, wp["w_mla"], wp["w_dil"], wp["q_norm"], wp["w_uq"], wp["kv_norm"], wp["w_uk"], wp["w_uv"], tabs)


def _mla_kernel(q_ref, k_ref, v_ref, o_ref, m_scr, l_scr, acc_scr, *, nk):
    ki = pl.program_id(3)

    @pl.when(ki == 0)
    def _():
        m_scr[...] = jnp.full(m_scr.shape, -jnp.inf, F32)
        l_scr[...] = jnp.zeros(l_scr.shape, F32)
        acc_scr[...] = jnp.zeros(acc_scr.shape, F32)

    v = v_ref[0]
    for hh in range(2):
        q = q_ref[0, :, hh * MLA_PAD:(hh + 1) * MLA_PAD]
        k = k_ref[0, :, hh * MLA_PAD:(hh + 1) * MLA_PAD]
        s = lax.dot_general(q, k, (((1,), (1,)), ((), ())), preferred_element_type=F32)
        m_prev = m_scr[hh]
        m_new = jnp.maximum(m_prev, jnp.max(s, axis=1, keepdims=True))
        alpha = jnp.exp(m_prev - m_new)
        p = jnp.exp(s - m_new)
        l_scr[hh] = alpha * l_scr[hh] + jnp.sum(p, axis=1, keepdims=True)
        acc_scr[hh] = alpha * acc_scr[hh] + jnp.dot(p.astype(BF16), v, preferred_element_type=F32)
        m_scr[hh] = m_new

    @pl.when(ki == nk - 1)
    def _():
        lane = lax.broadcasted_iota(jnp.int32, acc_scr.shape[1:], 1)
        o = jnp.where(lane < MLA_V, acc_scr[0] / l_scr[0], acc_scr[1] / l_scr[1])
        o_ref[0] = o.astype(BF16)


def _mla_attention(q, k, v, tq, tk):
    B, S, _ = q.shape
    nq, nk = S // tq, S // tk
    hp = MLA_HEADS // 2
    return pl.pallas_call(
        functools.partial(_mla_kernel, nk=nk),
        grid=(B, hp, nq, nk),
        in_specs=[
            pl.BlockSpec((1, tq, 2 * MLA_PAD), lambda b, h, i, j: (b, i, h)),
            pl.BlockSpec((1, tk, 2 * MLA_PAD), lambda b, h, i, j: (b, j, h)),
            pl.BlockSpec((1, tk, 2 * MLA_V), lambda b, h, i, j: (b, j, h)),
        ],
        out_specs=pl.BlockSpec((1, tq, 2 * MLA_V), lambda b, h, i, j: (b, i, h)),
        out_shape=jax.ShapeDtypeStruct((B, S, MLA_HEADS * MLA_V), BF16),
        scratch_shapes=[
            pltpu.VMEM((2, tq, 1), F32),
            pltpu.VMEM((2, tq, 1), F32),
            pltpu.VMEM((2, tq, 2 * MLA_V), F32),
        ],
        compiler_params=_cparams(("parallel", "parallel", "parallel", "arbitrary")),
    )(q, k, v)


def _dil_kernel(q_ref, kp_ref, kc_ref, kn_ref, vp_ref, vc_ref, vn_ref, o_ref, lse_ref, *, nt, tt):
    i = pl.program_id(2)
    kext = jnp.concatenate([kp_ref[0], kc_ref[0], kn_ref[0]], axis=0)
    vext = jnp.concatenate([vp_ref[0], vc_ref[0], vn_ref[0]], axis=0)
    nkx = tt + 2 * DIL_STEPS
    qi = lax.broadcasted_iota(jnp.int32, (tt, nkx), 0)
    kx = lax.broadcasted_iota(jnp.int32, (tt, nkx), 1)
    rel = kx - DIL_STEPS - qi
    ok = (rel >= -DIL_STEPS) & (rel <= DIL_STEPS)
    ok = ok & ((kx >= DIL_STEPS) | (i > 0)) & ((kx < tt + DIL_STEPS) | (i < nt - 1))
    lane_q = lax.broadcasted_iota(jnp.int32, (tt, LANES), 1)
    zero = jnp.zeros((tt, LANES), BF16)
    for hp in range(DIL_HEADS // 2):
        sl = slice(hp * LANES, (hp + 1) * LANES)
        qp = q_ref[0, :, sl]
        kp = kext[:, sl]
        vp = vext[:, sl]
        outs, lses = [], []
        for hh in range(2):
            mine = (lane_q < HEAD_DIM) if hh == 0 else (lane_q >= HEAD_DIM)
            qh = jnp.where(mine, qp, zero)
            s = lax.dot_general(qh, kp, (((1,), (1,)), ((), ())), preferred_element_type=F32)
            s = jnp.where(ok, s, NEG_BIG)
            m = jnp.max(s, axis=1, keepdims=True)
            e = jnp.exp(s - m)
            den = jnp.sum(e, axis=1, keepdims=True)
            o = jnp.dot(e.astype(BF16), vp, preferred_element_type=F32) / den
            outs.append(o)
            lses.append(jnp.broadcast_to(m + jnp.log(den), (tt, LANES)))
        first = lane_q < HEAD_DIM
        o_ref[0, :, sl] = jnp.where(first, outs[0], outs[1])
        lse_ref[0, :, sl] = jnp.where(first, lses[0], lses[1])


def _dil_attention(dqkv, g, dil, tt):
    B, S, _ = dqkv.shape
    L = S // dil
    nt = L // tt
    x = dqkv.reshape(B, L, dil * DIL_QKV)
    nblk = 3 * DIL_GROUPS
    hb = tt // DIL_STEPS
    nhb = L // DIL_STEPS
    cur = lambda c: pl.BlockSpec((1, tt, DIL_W), lambda b, r, i: (b, i, r * nblk + c))
    prev = lambda c: pl.BlockSpec((1, DIL_STEPS, DIL_W),
                                  lambda b, r, i: (b, jnp.maximum(i * hb - 1, 0), r * nblk + c))
    nxt = lambda c: pl.BlockSpec((1, DIL_STEPS, DIL_W),
                                 lambda b, r, i: (b, jnp.minimum((i + 1) * hb, nhb - 1), r * nblk + c))
    kc, vc = DIL_GROUPS + g, 2 * DIL_GROUPS + g
    out_spec = pl.BlockSpec((1, tt, DIL_W), lambda b, r, i: (b, i, r))
    o, lse = pl.pallas_call(
        functools.partial(_dil_kernel, nt=nt, tt=tt),
        grid=(B, dil, nt),
        in_specs=[cur(g), prev(kc), cur(kc), nxt(kc), prev(vc), cur(vc), nxt(vc)],
        out_specs=[out_spec, out_spec],
        out_shape=[jax.ShapeDtypeStruct((B, L, dil * DIL_W), F32)] * 2,
        compiler_params=_cparams(("parallel", "parallel", "parallel")),
    )(x, x, x, x, x, x, x)
    return o.reshape(B, S, DIL_W), lse.reshape(B, S, DIL_W)


def _ffn(x1, gf, wgu_ref, wdown_ref):
    h2 = _rms(x1, gf).astype(BF16)
    acc = x1
    for c in range(FF_CHUNKS):
        gate = jnp.dot(h2, wgu_ref[:, c * FF_CHUNK:(c + 1) * FF_CHUNK], preferred_element_type=F32)
        up = jnp.dot(h2, wgu_ref[:, D_FF + c * FF_CHUNK:D_FF + (c + 1) * FF_CHUNK],
                     preferred_element_type=F32)
        act = (gate * jax.nn.sigmoid(gate) * up).astype(BF16)
        acc = acc + jnp.dot(act, wdown_ref[c * FF_CHUNK:(c + 1) * FF_CHUNK, :], preferred_element_type=F32)
    return acc


def _out_ffn_attn_kernel(x_ref, om_ref, o0_ref, o1_ref, o2_ref, l0_ref, l1_ref, l2_ref,
                         wo_ref, gf_ref, wgu_ref, wdown_ref, y_ref):
    l0, l1, l2 = l0_ref[...], l1_ref[...], l2_ref[...]
    mx = jnp.maximum(jnp.maximum(l0, l1), l2)
    e0, e1, e2 = jnp.exp(l0 - mx), jnp.exp(l1 - mx), jnp.exp(l2 - mx)
    od = (e0 * o0_ref[...] + e1 * o1_ref[...] + e2 * o2_ref[...]) / (e0 + e1 + e2)
    o = jnp.concatenate([om_ref[...], od.astype(BF16)], axis=1)
    x1 = x_ref[...] + jnp.dot(o, wo_ref[...], preferred_element_type=F32)
    y_ref[...] = _ffn(x1, gf_ref[...], wgu_ref, wdown_ref)


def _out_ffn_rec_kernel(x_ref, a_ref, wo_ref, gf_ref, wgu_ref, wdown_ref, gfin_ref, y_ref):
    x1 = x_ref[...] + jnp.dot(a_ref[...], wo_ref[...], preferred_element_type=F32)
    y_ref[...] = _rms(_ffn(x1, gf_ref[...], wgu_ref, wdown_ref), gfin_ref[...])


def _out_ffn_attn(x, om, os_, ls_, wp, layer, tm):
    T, D = x.shape
    row = lambda w: pl.BlockSpec((tm, w), lambda i: (i, 0))
    return pl.pallas_call(
        _out_ffn_attn_kernel,
        grid=(T // tm,),
        in_specs=[row(D), row(DIL_W)] + [row(DIL_W)] * 6 + [
            _const_spec((MLA_HEADS * MLA_V + DIL_W, D)), _const_spec((1, D)),
            _const_spec((D, 2 * D_FF)), _const_spec((D_FF, D))],
        out_specs=row(D),
        out_shape=jax.ShapeDtypeStruct((T, D), F32),
        compiler_params=_cparams(("parallel",)),
    )(x, om, *os_, *ls_, wp["w_out_a"], wp["norm_ffn"][layer],
      wp["w_gu"][layer], wp["w_down"][layer])


def _out_ffn_rec(x, a, wp, layer, tm):
    T, D = x.shape
    row = lambda w: pl.BlockSpec((tm, w), lambda i: (i, 0))
    return pl.pallas_call(
        _out_ffn_rec_kernel,
        grid=(T // tm,),
        in_specs=[row(D), row(D_RNN), _const_spec((D_RNN, D)), _const_spec((1, D)),
                  _const_spec((D, 2 * D_FF)), _const_spec((D_FF, D)), _const_spec((1, D))],
        out_specs=row(D),
        out_shape=jax.ShapeDtypeStruct((T, D), F32),
        compiler_params=_cparams(("parallel",)),
    )(x, a, wp["w_out_r"], wp["norm_ffn"][layer], wp["w_gu"][layer], wp["w_down"][layer],
      wp["norm_final"])


def _l1_prep_kernel(x_ref, g_ref, w_ref, y_ref, xr_ref):
    h = _rms(x_ref[...], g_ref[...]).astype(BF16)
    zy = jnp.dot(h, w_ref[:, :D_RNN], preferred_element_type=F32)
    y_ref[...] = 0.5 * zy * (1.0 + jnp.tanh(np.sqrt(2.0 / np.pi).astype(np.float32)
                                            * (zy + 0.044715 * (zy * zy * zy))))
    xr_ref[...] = jnp.dot(h, w_ref[:, D_RNN:], preferred_element_type=F32)


def _l1_prep(x, g, w, tm):
    T, D = x.shape
    row = lambda w_: pl.BlockSpec((tm, w_), lambda i: (i, 0))
    return pl.pallas_call(
        _l1_prep_kernel,
        grid=(T // tm,),
        in_specs=[row(D), _const_spec((1, D)), _const_spec((D, 2 * D_RNN))],
        out_specs=[row(D_RNN), row(D_RNN)],
        out_shape=[jax.ShapeDtypeStruct((T, D_RNN), F32)] * 2,
        compiler_params=_cparams(("parallel",)),
    )(x, g, w)


def _rglru_kernel(*refs, ns, ts, cb, reverse):
    if reverse:
        (xp_ref, xc_ref, xn_ref, cw_ref, cbias_ref, wg_ref, br_ref, bi_ref, lam_ref, hf_ref, y_ref,
         o_ref, a_scr, u_scr, h_scr, carry_scr) = refs
    else:
        (xp_ref, xc_ref, xn_ref, cw_ref, cbias_ref, wg_ref, br_ref, bi_ref, lam_ref,
         o_ref, a_scr, u_scr, carry_scr) = refs
        h_scr = o_ref.at[0]
    step = pl.program_id(2)
    tile = (ns - 1 - step) if reverse else step

    @pl.when(step == 0)
    def _():
        carry_scr[...] = jnp.zeros(carry_scr.shape, F32)

    x0 = xc_ref[0]
    rows = lax.broadcasted_iota(jnp.int32, (ts, cb), 0)
    pv = jnp.where(tile > 0, xp_ref[0], 0.0)
    nv = jnp.where(tile < ns - 1, xn_ref[0], 0.0)
    xm1 = jnp.where(rows == 0, pv[SUBLANES - 1:SUBLANES], pltpu.roll(x0, 1, 0))
    xm2 = pltpu.roll(x0, 2, 0)
    xm2 = jnp.where(rows == 0, pv[SUBLANES - 2:SUBLANES - 1], xm2)
    xm2 = jnp.where(rows == 1, pv[SUBLANES - 1:SUBLANES], xm2)
    xp1 = jnp.where(rows == ts - 1, nv[0:1], pltpu.roll(x0, ts - 1, 0))
    xc = (xm2 * cw_ref[0:1] + xm1 * cw_ref[1:2] + x0 * cw_ref[2:3] + xp1 * cw_ref[3:4]) + cbias_ref[...]

    xcb = xc.astype(BF16)
    for j in range(cb // LRU_BW):
        sl = slice(j * LRU_BW, (j + 1) * LRU_BW)
        gts = jnp.dot(xcb[:, sl], wg_ref[j], preferred_element_type=F32)
        r = jax.nn.sigmoid(gts[:, :LRU_BW] + br_ref[:, sl])
        ig = jax.nn.sigmoid(gts[:, LRU_BW:] + bi_ref[:, sl])
        nlam = -lam_ref[:, sl]
        softplus = jnp.maximum(nlam, 0.0) + jnp.log1p(jnp.exp(-jnp.abs(nlam)))
        a = jnp.exp((-LRU_C) * r * softplus)
        a_scr[:, sl] = a
        u_scr[:, sl] = jnp.sqrt(jnp.maximum(1.0 - a * a, 0.0)) * (ig * xc[:, sl])

    row8 = lax.broadcasted_iota(jnp.int32, (SUBLANES, cb), 0)
    nchunk = ts // SUBLANES

    def chunk(c, hprev):
        idx = (nchunk - 1 - c) if reverse else c
        r0 = pl.multiple_of(idx * SUBLANES, SUBLANES)
        a = a_scr[pl.ds(r0, SUBLANES), :]
        b = u_scr[pl.ds(r0, SUBLANES), :]
        for d in (1, 2, 4):
            if reverse:
                keep = row8 < SUBLANES - d
                sh = SUBLANES - d
            else:
                keep = row8 >= d
                sh = d
            a_s = jnp.where(keep, pltpu.roll(a, sh, 0), 1.0)
            b_s = jnp.where(keep, pltpu.roll(b, sh, 0), 0.0)
            b = a * b_s + b
            a = a * a_s
        h = a * hprev + b
        h_scr[pl.ds(r0, SUBLANES), :] = h
        edge = h[0:1] if reverse else h[SUBLANES - 1:SUBLANES]
        return jnp.broadcast_to(edge, (SUBLANES, cb))

    carry_scr[...] = lax.fori_loop(0, nchunk, chunk, carry_scr[...], unroll=4)

    if reverse:
        o_ref[0] = ((hf_ref[0] + h_scr[...]) * y_ref[0]).astype(BF16)


def _rglru(xr, wp, ts, cb, reverse, hf=None, y=None):
    B, S, C = xr.shape
    ns = S // ts
    nhb = S // SUBLANES
    hb = ts // SUBLANES
    d = 1 if reverse else 0
    tidx = (lambda i: ns - 1 - i) if reverse else (lambda i: i)
    cur = pl.BlockSpec((1, ts, cb), lambda b, c, i: (b, tidx(i), c))
    prev = pl.BlockSpec((1, SUBLANES, cb), lambda b, c, i: (b, jnp.maximum(tidx(i) * hb - 1, 0), c))
    nxt = pl.BlockSpec((1, SUBLANES, cb), lambda b, c, i: (b, jnp.minimum((tidx(i) + 1) * hb, nhb - 1), c))
    vec = pl.BlockSpec((1, cb), lambda b, c, i: (0, c))
    in_specs = [prev, cur, nxt,
                pl.BlockSpec((CONV_W, cb), lambda b, c, i: (0, c)), vec,
                pl.BlockSpec((cb // LRU_BW, LRU_BW, 2 * LRU_BW), lambda b, c, i: (c, 0, 0)),
                vec, vec, vec]
    args = [xr, xr, xr, wp["conv_w"], wp["conv_b"], wp["w_gate"][d], wp["b_r"][d], wp["b_i"][d],
            wp["lam"][d]]
    scratch = [pltpu.VMEM((ts, cb), F32), pltpu.VMEM((ts, cb), F32)]
    if reverse:
        in_specs += [cur, cur]
        args += [hf, y]
        scratch.append(pltpu.VMEM((ts, cb), F32))
        out_dtype = BF16
    else:
        out_dtype = F32
    scratch.append(pltpu.VMEM((SUBLANES, cb), F32))
    return pl.pallas_call(
        functools.partial(_rglru_kernel, ns=ns, ts=ts, cb=cb, reverse=reverse),
        grid=(B, C // cb, ns),
        in_specs=in_specs,
        out_specs=cur,
        out_shape=jax.ShapeDtypeStruct((B, S, C), out_dtype),
        scratch_shapes=scratch,
        compiler_params=_cparams(("parallel", "parallel", "arbitrary")),
    )(*args)


def _lane_table(vals, jidx, mask, fill):
    return jnp.where(mask[None, :], vals[:, jidx], fill)


def _rope_tables(S, half, period, start, scale):
    inv = jnp.power(ROPE_THETA, -jnp.arange(half, dtype=F32) / half)
    ang = jnp.arange(S, dtype=F32)[:, None] * inv[None, :]
    cos, sin = jnp.cos(ang), jnp.sin(ang)
    e = np.arange(LANES) % period - start
    first = (e >= 0) & (e < half)
    second = (e >= half) & (e < 2 * half)
    jidx = np.where(first | second, e % half, 0)
    c = _lane_table(cos, jidx, first | second, 1.0)
    s1 = _lane_table(-sin, jidx, first, 0.0)
    s2 = _lane_table(sin, jidx, second, 0.0)
    return jnp.stack([c, s1, s2]) * scale


def _all_tables(S):
    mla_scale = float((MLA_NOPE + MLA_ROPE) ** -0.5)
    dil_scale = float(HEAD_DIM ** -0.5)
    return jnp.concatenate([
        _rope_tables(S, MLA_ROPE // 2, MLA_PAD, MLA_NOPE, mla_scale),
        _rope_tables(S, MLA_ROPE // 2, MLA_PAD, MLA_NOPE, 1.0),
        _rope_tables(S, DIL_ROT // 2, HEAD_DIM, 0, dil_scale),
        _rope_tables(S, DIL_ROT // 2, HEAD_DIM, 0, 1.0),
    ])


def _prep_weights(norm_mix, w_in_a, q_norm, w_uq, kv_norm, w_ukv, w_out_a, w_in_r, conv_w, conv_b,
                  lru_w_gate, lru_b_gate, lru_lambda, w_out_r, norm_ffn, w_gu, w_down, norm_final):
    w_in = w_in_a[0]
    qk_dim = MLA_NOPE + MLA_ROPE
    kr_cols = jnp.pad(w_in[:, MLA_Q_RANK + MLA_KV_RANK:MLA_IN],
                      ((0, 0), (MLA_NOPE, MLA_PAD - qk_dim)))
    w_mla = jnp.concatenate([w_in[:, :MLA_Q_RANK + MLA_KV_RANK], kr_cols], axis=1)
    uq = w_uq[0].reshape(MLA_Q_RANK, MLA_HEADS, qk_dim)
    uq = jnp.pad(uq, ((0, 0), (0, 0), (0, MLA_PAD - qk_dim))).reshape(MLA_Q_RANK, MLA_HEADS * MLA_PAD)
    ukv = w_ukv[0].reshape(MLA_KV_RANK, MLA_HEADS, MLA_NOPE + MLA_V)
    uk = jnp.pad(ukv[:, :, :MLA_NOPE], ((0, 0), (0, 0), (0, MLA_PAD - MLA_NOPE)))
    uk = uk.reshape(MLA_KV_RANK, MLA_HEADS * MLA_PAD)
    uv = ukv[:, :, MLA_NOPE:].reshape(MLA_KV_RANK, MLA_HEADS * MLA_V)
    wg = lru_w_gate[0]
    w_gate = jnp.concatenate([wg[:, 0], wg[:, 1]], axis=-1)
    row = lambda v: v.reshape(1, -1)
    return {
        "norm_mix": [row(norm_mix[l]) for l in range(2)],
        "w_mla": w_mla.astype(BF16),
        "w_dil": w_in[:, MLA_IN:].astype(BF16),
        "q_norm": row(q_norm[0]),
        "w_uq": uq.astype(BF16),
        "kv_norm": row(kv_norm[0]),
        "w_uk": uk.astype(BF16),
        "w_uv": uv.astype(BF16),
        "w_out_a": w_out_a[0].astype(BF16),
        "w_in_r": w_in_r[0].astype(BF16),
        "conv_w": conv_w[0],
        "conv_b": row(conv_b[0]),
        "w_gate": w_gate.astype(BF16),
        "b_r": [row(lru_b_gate[0, d, 0]) for d in range(2)],
        "b_i": [row(lru_b_gate[0, d, 1]) for d in range(2)],
        "lam": [row(lru_lambda[0, d]) for d in range(2)],
        "w_out_r": w_out_r[0].astype(BF16),
        "norm_ffn": [row(norm_ffn[l]) for l in range(2)],
        "w_gu": [w_gu[l].astype(BF16) for l in range(2)],
        "w_down": [w_down[l].astype(BF16) for l in range(2)],
        "norm_final": row(norm_final),
    }


def _trunk(x, wp):
    B, S, D = x.shape
    T = B * S
    tabs = _all_tables(S)
    q, k, v, dqkv = _l0_prep(x, wp["norm_mix"][0], wp, tabs, tm=256)
    o_mla = _mla_attention(q, k, v, tq=512, tk=512)
    os_, ls_ = [], []
    for g, (_, dil) in enumerate(DIL_PATTERNS):
        o_g, l_g = _dil_attention(dqkv, g, dil, tt=128)
        os_.append(o_g.reshape(T, DIL_W))
        ls_.append(l_g.reshape(T, DIL_W))
    x1 = _out_ffn_attn(x.reshape(T, D), o_mla.reshape(T, MLA_HEADS * MLA_V), os_, ls_, wp, 0, tm=256)
    y, xr = _l1_prep(x1, wp["norm_mix"][1], wp["w_in_r"], tm=256)
    xr = xr.reshape(B, S, D_RNN)
    hf = _rglru(xr, wp, ts=256, cb=512, reverse=False)
    a = _rglru(xr, wp, ts=256, cb=512, reverse=True, hf=hf, y=y.reshape(B, S, D_RNN))
    out = _out_ffn_rec(x1, a.reshape(T, D_RNN), wp, 1, tm=256)
    return out.reshape(B, S, D)


def kernel(x_prompt, x_sample, norm_mix, w_in_a, q_norm, w_uq, kv_norm, w_ukv, w_out_a, w_in_r, conv_w,
           conv_b, lru_w_gate, lru_b_gate, lru_lambda, w_out_r, norm_ffn, w_gu, w_down, norm_final):
    wp = _prep_weights(norm_mix, w_in_a, q_norm, w_uq, kv_norm, w_ukv, w_out_a, w_in_r, conv_w, conv_b,
                       lru_w_gate, lru_b_gate, lru_lambda, w_out_r, norm_ffn, w_gu, w_down, norm_final)
    return (_trunk(x_prompt, wp), _trunk(x_sample, wp))
```

```python
import functools

import numpy as np
import jax
import jax.numpy as jnp
from jax import lax
from jax.experimental import pallas as pl
from jax.experimental.pallas import tpu as pltpu

F32 = jnp.float32
BF16 = jnp.bfloat16

D_MODEL = 1024
HEAD_DIM = 64
ROPE_THETA = 500000.0
NORM_EPS = 1e-6
MLA_HEADS = 8
MLA_NOPE = 64
MLA_ROPE = 32
MLA_V = 64
MLA_Q_RANK = 256
MLA_KV_RANK = 128
DIL_PATTERNS = ((128, 1), (512, 4), (2048, 16))
DIL_GROUPS = len(DIL_PATTERNS)
DIL_HEADS = 8
DIL_ROT = HEAD_DIM // 4
DIL_STEPS = 64
MLA_IN = MLA_Q_RANK + MLA_KV_RANK + MLA_ROPE
DIL_W = DIL_HEADS * HEAD_DIM
DIL_QKV = 3 * DIL_GROUPS * DIL_W
D_RNN = 1536
LRU_BLOCKS = 12
LRU_BW = D_RNN // LRU_BLOCKS
CONV_W = 4
CONV_LEFT = 2
LRU_C = 8.0
D_FF = ((8 * D_MODEL // 3 + 255) // 256) * 256
NEG_BIG = -1e30

LANES = 128
SUBLANES = 8
MLA_PAD = 128
VMEM_LIMIT = 56 * 1024 * 1024

FF_CHUNKS = 2
FF_CHUNK = D_FF // FF_CHUNKS


def _cparams(sem):
    return pltpu.CompilerParams(dimension_semantics=sem, vmem_limit_bytes=VMEM_LIMIT)


def _const_spec(shape):
    nd = len(shape)
    return pl.BlockSpec(shape, lambda *_: (0,) * nd, pipeline_mode=pl.Buffered(1))


def _rms(x, g):
    return x * lax.rsqrt(jnp.mean(x * x, axis=-1, keepdims=True) + NORM_EPS) * g


def _rope_block(xb, c, s1, s2, half):
    return xb * c + pltpu.roll(xb, LANES - half, 1) * s1 + pltpu.roll(xb, half, 1) * s2


def _l0_prep_kernel(x_ref, g_ref, wmla_ref, wdil_ref, qn_ref, wuq_ref, kvn_ref, wuk_ref, wuv_ref,
                    tab_ref, q_out, k_out, v_out, d0_out, d1_out, d2_out, perm_scr):
    tm = x_ref.shape[1]
    d_outs = (d0_out, d1_out, d2_out)
    x = x_ref[0]
    h = _rms(x, g_ref[...]).astype(BF16)
    z = jnp.dot(h, wmla_ref[...], preferred_element_type=F32)
    qn = _rms(z[:, :MLA_Q_RANK], qn_ref[...]).astype(BF16)
    kvn = _rms(z[:, MLA_Q_RANK:MLA_Q_RANK + MLA_KV_RANK], kvn_ref[...]).astype(BF16)
    kr = z[:, MLA_Q_RANK + MLA_KV_RANK:]
    qf = jnp.dot(qn, wuq_ref[...], preferred_element_type=F32)
    kf = jnp.dot(kvn, wuk_ref[...], preferred_element_type=F32)
    vt = lax.dot_general(wuv_ref[...], kvn, (((1,), (1,)), ((), ())),
                         preferred_element_type=F32)
    half = MLA_ROPE // 2
    krr = _rope_block(kr, tab_ref[3], tab_ref[4], tab_ref[5], half)
    for hh in range(MLA_HEADS):
        sl = slice(hh * MLA_PAD, (hh + 1) * MLA_PAD)
        q_out[0, :, sl] = _rope_block(qf[:, sl], tab_ref[0], tab_ref[1], tab_ref[2], half).astype(BF16)
        k_out[0, :, sl] = (kf[:, sl] + krr).astype(BF16)
    v_out[0] = vt.astype(BF16)
    dhalf = DIL_ROT // 2
    for j in range(3 * DIL_GROUPS):
        c, g = divmod(j, DIL_GROUPS)
        dil = DIL_PATTERNS[g][1]
        zc = jnp.dot(h, wdil_ref[:, j * DIL_W:(j + 1) * DIL_W], preferred_element_type=F32)
        for b in range(DIL_W // LANES):
            zb = zc[:, b * LANES:(b + 1) * LANES]
            if c < 2:
                t0 = 6 + 3 * c
                zb = _rope_block(zb, tab_ref[t0], tab_ref[t0 + 1], tab_ref[t0 + 2], dhalf)
            if dil == 1:
                d_outs[g][0, :, c * DIL_W + b * LANES:c * DIL_W + (b + 1) * LANES] = zb.astype(BF16)
            else:
                perm_scr[j, b] = zb
        if dil > 1:
            for r in range(dil):
                for b in range(DIL_W // LANES):
                    col = (r * 3 + c) * DIL_W + b * LANES
                    d_outs[g][0, :, col:col + LANES] = perm_scr[
                        j, b, pl.ds(r, tm // dil, stride=dil), :].astype(BF16)


def _l0_prep(x, g, wp, tabs, tm):
    B, S, D = x.shape
    grid = (B, S // tm)
    row = lambda w: pl.BlockSpec((1, tm, w), lambda b, i: (b, i, 0))
    return pl.pallas_call(
        _l0_prep_kernel,
        grid=grid,
        in_specs=[
            row(D),
            _const_spec((1, D)),
            _const_spec(wp["w_mla"].shape),
            _const_spec(wp["w_dil"].shape),
            _const_spec((1, MLA_Q_RANK)),
            _const_spec(wp["w_uq"].shape),
            _const_spec((1, MLA_KV_RANK)),
            _const_spec(wp["w_uk"].shape),
            _const_spec(wp["w_uv"].shape),
            pl.BlockSpec((12, tm, LANES), lambda b, i: (0, i, 0)),
        ],
        out_specs=[row(MLA_HEADS * MLA_PAD), row(MLA_HEADS * MLA_PAD),
                   pl.BlockSpec((1, MLA_HEADS * MLA_V, tm), lambda b, i: (b, 0, i))] + [
            pl.BlockSpec((1, tm // d, d * 3 * DIL_W), lambda b, i: (b, i, 0)) for _, d in DIL_PATTERNS],
        out_shape=[
            jax.ShapeDtypeStruct((B, S, MLA_HEADS * MLA_PAD), BF16),
            jax.ShapeDtypeStruct((B, S, MLA_HEADS * MLA_PAD), BF16),
            jax.ShapeDtypeStruct((B, MLA_HEADS * MLA_V, S), BF16),
        ] + [jax.ShapeDtypeStruct((B, S // d, d * 3 * DIL_W), BF16) for _, d in DIL_PATTERNS],
        scratch_shapes=[pltpu.VMEM((3 * DIL_GROUPS, DIL_W // LANES, tm, LANES), F32)],
        compiler_params=_cparams(("parallel", "parallel")),
    )(x, g, wp["w_mla"], wp["w_dil"], wp["q_norm"], wp["w_uq"], wp["kv_norm"], wp["w_uk"], wp["w_uv"], tabs)


def _mla_kernel(q_ref, k_ref, vt_ref, o_ref, m_scr, l_scr, acc_scr, *, nk):
    ki = pl.program_id(3)

    @pl.when(ki == 0)
    def _():
        m_scr[...] = jnp.full(m_scr.shape, -jnp.inf, F32)
        l_scr[...] = jnp.zeros(l_scr.shape, F32)
        acc_scr[...] = jnp.zeros(acc_scr.shape, F32)

    vt = vt_ref[0]
    for hh in range(2):
        q = q_ref[0, :, hh * MLA_PAD:(hh + 1) * MLA_PAD]
        k = k_ref[0, :, hh * MLA_PAD:(hh + 1) * MLA_PAD]
        st = lax.dot_general(k, q, (((1,), (1,)), ((), ())), preferred_element_type=F32)
        m_prev = m_scr[hh]
        m_new = jnp.maximum(m_prev, jnp.max(st, axis=0, keepdims=True))
        alpha = jnp.exp2(m_prev - m_new)
        p = jnp.exp2(st - m_new)
        l_scr[hh] = alpha * l_scr[hh] + jnp.sum(p, axis=0, keepdims=True)
        acc_scr[hh] = alpha * acc_scr[hh] + jnp.dot(vt, p.astype(BF16), preferred_element_type=F32)
        m_scr[hh] = m_new

    @pl.when(ki == nk - 1)
    def _():
        row = lax.broadcasted_iota(jnp.int32, acc_scr.shape[1:], 0)
        ot = jnp.where(row < MLA_V, acc_scr[0] / l_scr[0], acc_scr[1] / l_scr[1])
        o_ref[0] = ot.T.astype(BF16)


def _mla_attention(q, k, vt, tq, tk):
    B, S, _ = q.shape
    nq, nk = S // tq, S // tk
    hp = MLA_HEADS // 2
    return pl.pallas_call(
        functools.partial(_mla_kernel, nk=nk),
        grid=(B, hp, nq, nk),
        in_specs=[
            pl.BlockSpec((1, tq, 2 * MLA_PAD), lambda b, h, i, j: (b, i, h)),
            pl.BlockSpec((1, tk, 2 * MLA_PAD), lambda b, h, i, j: (b, j, h)),
            pl.BlockSpec((1, 2 * MLA_V, tk), lambda b, h, i, j: (b, h, j)),
        ],
        out_specs=pl.BlockSpec((1, tq, 2 * MLA_V), lambda b, h, i, j: (b, i, h)),
        out_shape=jax.ShapeDtypeStruct((B, S, MLA_HEADS * MLA_V), BF16),
        scratch_shapes=[
            pltpu.VMEM((2, 1, tq), F32),
            pltpu.VMEM((2, 1, tq), F32),
            pltpu.VMEM((2, 2 * MLA_V, tq), F32),
        ],
        compiler_params=_cparams(("parallel", "parallel", "parallel", "arbitrary")),
    )(q, k, vt)


def _dil_kernel(q_ref, kp_ref, kc_ref, kn_ref, vp_ref, vc_ref, vn_ref, o_ref, lse_ref, *, nt, tt):
    i = pl.program_id(2)
    kext = jnp.concatenate([kp_ref[0], kc_ref[0], kn_ref[0]], axis=0)
    vext = jnp.concatenate([vp_ref[0], vc_ref[0], vn_ref[0]], axis=0)
    nkx = tt + 2 * DIL_STEPS
    qi = lax.broadcasted_iota(jnp.int32, (tt, nkx), 0)
    kx = lax.broadcasted_iota(jnp.int32, (tt, nkx), 1)
    rel = kx - DIL_STEPS - qi
    ok = (rel >= -DIL_STEPS) & (rel <= DIL_STEPS)
    ok = ok & ((kx >= DIL_STEPS) | (i > 0)) & ((kx < tt + DIL_STEPS) | (i < nt - 1))
    lane_q = lax.broadcasted_iota(jnp.int32, (tt, LANES), 1)
    zero = jnp.zeros((tt, LANES), BF16)
    for hp in range(DIL_HEADS // 2):
        sl = slice(hp * LANES, (hp + 1) * LANES)
        qp = q_ref[0, :, sl]
        kp = kext[:, sl]
        vp = vext[:, sl]
        outs, lses = [], []
        for hh in range(2):
            mine = (lane_q < HEAD_DIM) if hh == 0 else (lane_q >= HEAD_DIM)
            qh = jnp.where(mine, qp, zero)
            s = lax.dot_general(qh, kp, (((1,), (1,)), ((), ())), preferred_element_type=F32)
            s = jnp.where(ok, s, NEG_BIG)
            m = jnp.max(s, axis=1, keepdims=True)
            e = jnp.exp(s - m)
            den = jnp.sum(e, axis=1, keepdims=True)
            o = jnp.dot(e.astype(BF16), vp, preferred_element_type=F32) / den
            outs.append(o)
            lses.append(jnp.broadcast_to(m + jnp.log(den), (tt, LANES)))
        first = lane_q < HEAD_DIM
        o_ref[0, :, sl] = jnp.where(first, outs[0], outs[1])
        lse_ref[0, :, sl] = jnp.where(first, lses[0], lses[1])


def _dil_attention(x, dil, tt):
    B, L, _ = x.shape
    nt = L // tt
    nblk = 3
    hb = tt // DIL_STEPS
    nhb = L // DIL_STEPS
    cur = lambda c: pl.BlockSpec((1, tt, DIL_W), lambda b, r, i: (b, i, r * nblk + c))
    prev = lambda c: pl.BlockSpec((1, DIL_STEPS, DIL_W),
                                  lambda b, r, i: (b, jnp.maximum(i * hb - 1, 0), r * nblk + c))
    nxt = lambda c: pl.BlockSpec((1, DIL_STEPS, DIL_W),
                                 lambda b, r, i: (b, jnp.minimum((i + 1) * hb, nhb - 1), r * nblk + c))
    out_spec = pl.BlockSpec((1, tt, DIL_W), lambda b, r, i: (b, i, r))
    return pl.pallas_call(
        functools.partial(_dil_kernel, nt=nt, tt=tt),
        grid=(B, dil, nt),
        in_specs=[cur(0), prev(1), cur(1), nxt(1), prev(2), cur(2), nxt(2)],
        out_specs=[out_spec, out_spec],
        out_shape=[jax.ShapeDtypeStruct((B, L, dil * DIL_W), F32)] * 2,
        compiler_params=_cparams(("parallel", "parallel", "parallel")),
    )(x, x, x, x, x, x, x)


def _ffn(x1, gf, wgu_ref, wdown_ref):
    h2 = _rms(x1, gf).astype(BF16)
    acc = x1
    for c in range(FF_CHUNKS):
        gate = jnp.dot(h2, wgu_ref[:, c * FF_CHUNK:(c + 1) * FF_CHUNK], preferred_element_type=F32)
        up = jnp.dot(h2, wgu_ref[:, D_FF + c * FF_CHUNK:D_FF + (c + 1) * FF_CHUNK],
                     preferred_element_type=F32)
        act = (gate * jax.nn.sigmoid(gate) * up).astype(BF16)
        acc = acc + jnp.dot(act, wdown_ref[c * FF_CHUNK:(c + 1) * FF_CHUNK, :], preferred_element_type=F32)
    return acc


def _out_ffn_attn_kernel(x_ref, om_ref, o0_ref, o1_ref, o2_ref, l0_ref, l1_ref, l2_ref,
                         wo_ref, gf_ref, wgu_ref, wdown_ref, y_ref, perm_scr):
    tm = x_ref.shape[0]

    def natural(ref, slot, dil):
        for r in range(dil):
            for b in range(DIL_W // LANES):
                perm_scr[slot, b, pl.ds(r, tm // dil, stride=dil), :] = ref[
                    :, r * DIL_W + b * LANES:r * DIL_W + (b + 1) * LANES]
        return jnp.concatenate([perm_scr[slot, b] for b in range(DIL_W // LANES)], axis=1)

    d1, d2 = DIL_PATTERNS[1][1], DIL_PATTERNS[2][1]
    l0, l1, l2 = l0_ref[...], natural(l1_ref, 0, d1), natural(l2_ref, 1, d2)
    mx = jnp.maximum(jnp.maximum(l0, l1), l2)
    e0, e1, e2 = jnp.exp(l0 - mx), jnp.exp(l1 - mx), jnp.exp(l2 - mx)
    od = (e0 * o0_ref[...] + e1 * natural(o1_ref, 2, d1) + e2 * natural(o2_ref, 3, d2)) / (e0 + e1 + e2)
    o = jnp.concatenate([om_ref[...], od.astype(BF16)], axis=1)
    x1 = x_ref[...] + jnp.dot(o, wo_ref[...], preferred_element_type=F32)
    y_ref[...] = _ffn(x1, gf_ref[...], wgu_ref, wdown_ref)


def _out_ffn_rec_kernel(x_ref, a_ref, wo_ref, gf_ref, wgu_ref, wdown_ref, gfin_ref, y_ref):
    x1 = x_ref[...] + jnp.dot(a_ref[...], wo_ref[...], preferred_element_type=F32)
    y_ref[...] = _rms(_ffn(x1, gf_ref[...], wgu_ref, wdown_ref), gfin_ref[...])


def _out_ffn_attn(x, om, os_, ls_, wp, layer, tm):
    T, D = x.shape
    row = lambda w: pl.BlockSpec((tm, w), lambda i: (i, 0))
    grp = [pl.BlockSpec((tm // d, d * DIL_W), lambda i: (i, 0)) for _, d in DIL_PATTERNS]
    return pl.pallas_call(
        _out_ffn_attn_kernel,
        grid=(T // tm,),
        in_specs=[row(D), row(DIL_W)] + grp + grp + [
            _const_spec((MLA_HEADS * MLA_V + DIL_W, D)), _const_spec((1, D)),
            _const_spec((D, 2 * D_FF)), _const_spec((D_FF, D))],
        out_specs=row(D),
        out_shape=jax.ShapeDtypeStruct((T, D), F32),
        scratch_shapes=[pltpu.VMEM((4, DIL_W // LANES, tm, LANES), F32)],
        compiler_params=_cparams(("parallel",)),
    )(x, om, *os_, *ls_, wp["w_out_a"], wp["norm_ffn"][layer],
      wp["w_gu"][layer], wp["w_down"][layer])


def _out_ffn_rec(x, a, wp, layer, tm):
    T, D = x.shape
    row = lambda w: pl.BlockSpec((tm, w), lambda i: (i, 0))
    return pl.pallas_call(
        _out_ffn_rec_kernel,
        grid=(T // tm,),
        in_specs=[row(D), row(D_RNN), _const_spec((D_RNN, D)), _const_spec((1, D)),
                  _const_spec((D, 2 * D_FF)), _const_spec((D_FF, D)), _const_spec((1, D))],
        out_specs=row(D),
        out_shape=jax.ShapeDtypeStruct((T, D), F32),
        compiler_params=_cparams(("parallel",)),
    )(x, a, wp["w_out_r"], wp["norm_ffn"][layer], wp["w_gu"][layer], wp["w_down"][layer],
      wp["norm_final"])


def _l1_prep_kernel(x_ref, g_ref, w_ref, y_ref, xr_ref):
    h = _rms(x_ref[...], g_ref[...]).astype(BF16)
    zy = jnp.dot(h, w_ref[:, :D_RNN], preferred_element_type=F32)
    y_ref[...] = 0.5 * zy * (1.0 + jnp.tanh(np.sqrt(2.0 / np.pi).astype(np.float32)
                                            * (zy + 0.044715 * (zy * zy * zy))))
    xr_ref[...] = jnp.dot(h, w_ref[:, D_RNN:], preferred_element_type=F32)


def _l1_prep(x, g, w, tm):
    T, D = x.shape
    row = lambda w_: pl.BlockSpec((tm, w_), lambda i: (i, 0))
    return pl.pallas_call(
        _l1_prep_kernel,
        grid=(T // tm,),
        in_specs=[row(D), _const_spec((1, D)), _const_spec((D, 2 * D_RNN))],
        out_specs=[row(D_RNN), row(D_RNN)],
        out_shape=[jax.ShapeDtypeStruct((T, D_RNN), F32)] * 2,
        compiler_params=_cparams(("parallel",)),
    )(x, g, w)


def _rglru_kernel(*refs, ns, ts, cb, reverse):
    if reverse:
        (xp_ref, xc_ref, xn_ref, cw_ref, cbias_ref, wg_ref, br_ref, bi_ref, lam_ref, hf_ref, y_ref,
         o_ref, a_scr, u_scr, h_scr, carry_scr) = refs
    else:
        (xp_ref, xc_ref, xn_ref, cw_ref, cbias_ref, wg_ref, br_ref, bi_ref, lam_ref,
         o_ref, a_scr, u_scr, carry_scr) = refs
        h_scr = o_ref.at[0]
    step = pl.program_id(2)
    tile = (ns - 1 - step) if reverse else step

    @pl.when(step == 0)
    def _():
        carry_scr[...] = jnp.zeros(carry_scr.shape, F32)

    x0 = xc_ref[0]
    rows = lax.broadcasted_iota(jnp.int32, (ts, cb), 0)
    pv = jnp.where(tile > 0, xp_ref[0], 0.0)
    nv = jnp.where(tile < ns - 1, xn_ref[0], 0.0)
    xm1 = jnp.where(rows == 0, pv[SUBLANES - 1:SUBLANES], pltpu.roll(x0, 1, 0))
    xm2 = pltpu.roll(x0, 2, 0)
    xm2 = jnp.where(rows == 0, pv[SUBLANES - 2:SUBLANES - 1], xm2)
    xm2 = jnp.where(rows == 1, pv[SUBLANES - 1:SUBLANES], xm2)
    xp1 = jnp.where(rows == ts - 1, nv[0:1], pltpu.roll(x0, ts - 1, 0))
    xc = (xm2 * cw_ref[0:1] + xm1 * cw_ref[1:2] + x0 * cw_ref[2:3] + xp1 * cw_ref[3:4]) + cbias_ref[...]

    xcb = xc.astype(BF16)
    for j in range(cb // LRU_BW):
        sl = slice(j * LRU_BW, (j + 1) * LRU_BW)
        gts = jnp.dot(xcb[:, sl], wg_ref[j], preferred_element_type=F32)
        r = jax.nn.sigmoid(gts[:, :LRU_BW] + br_ref[:, sl])
        ig = jax.nn.sigmoid(gts[:, LRU_BW:] + bi_ref[:, sl])
        nlam = -lam_ref[:, sl]
        softplus = jnp.maximum(nlam, 0.0) + jnp.log1p(jnp.exp(-jnp.abs(nlam)))
        a = jnp.exp((-LRU_C) * r * softplus)
        a_scr[:, sl] = a
        u_scr[:, sl] = jnp.sqrt(jnp.maximum(1.0 - a * a, 0.0)) * (ig * xc[:, sl])

    row8 = lax.broadcasted_iota(jnp.int32, (SUBLANES, cb), 0)
    nchunk = ts // SUBLANES

    def chunk(c, hprev):
        idx = (nchunk - 1 - c) if reverse else c
        r0 = pl.multiple_of(idx * SUBLANES, SUBLANES)
        a = a_scr[pl.ds(r0, SUBLANES), :]
        b = u_scr[pl.ds(r0, SUBLANES), :]
        for d in (1, 2, 4):
            if reverse:
                keep = row8 < SUBLANES - d
                sh = SUBLANES - d
            else:
                keep = row8 >= d
                sh = d
            a_s = jnp.where(keep, pltpu.roll(a, sh, 0), 1.0)
            b_s = jnp.where(keep, pltpu.roll(b, sh, 0), 0.0)
            b = a * b_s + b
            a = a * a_s
        h = a * hprev + b
        h_scr[pl.ds(r0, SUBLANES), :] = h
        edge = h[0:1] if reverse else h[SUBLANES - 1:SUBLANES]
        return jnp.broadcast_to(edge, (SUBLANES, cb))

    carry_scr[...] = lax.fori_loop(0, nchunk, chunk, carry_scr[...], unroll=4)

    if reverse:
        o_ref[0] = ((hf_ref[0] + h_scr[...]) * y_ref[0]).astype(BF16)


def _rglru(xr, wp, ts, cb, reverse, hf=None, y=None):
    B, S, C = xr.shape
    ns = S // ts
    nhb = S // SUBLANES
    hb = ts // SUBLANES
    d = 1 if reverse else 0
    tidx = (lambda i: ns - 1 - i) if reverse else (lambda i: i)
    cur = pl.BlockSpec((1, ts, cb), lambda b, c, i: (b, tidx(i), c))
    prev = pl.BlockSpec((1, SUBLANES, cb), lambda b, c, i: (b, jnp.maximum(tidx(i) * hb - 1, 0), c))
    nxt = pl.BlockSpec((1, SUBLANES, cb), lambda b, c, i: (b, jnp.minimum((tidx(i) + 1) * hb, nhb - 1), c))
    vec = pl.BlockSpec((1, cb), lambda b, c, i: (0, c))
    in_specs = [prev, cur, nxt,
                pl.BlockSpec((CONV_W, cb), lambda b, c, i: (0, c)), vec,
                pl.BlockSpec((cb // LRU_BW, LRU_BW, 2 * LRU_BW), lambda b, c, i: (c, 0, 0)),
                vec, vec, vec]
    args = [xr, xr, xr, wp["conv_w"], wp["conv_b"], wp["w_gate"][d], wp["b_r"][d], wp["b_i"][d],
            wp["lam"][d]]
    scratch = [pltpu.VMEM((ts, cb), F32), pltpu.VMEM((ts, cb), F32)]
    if reverse:
        in_specs += [cur, cur]
        args += [hf, y]
        scratch.append(pltpu.VMEM((ts, cb), F32))
        out_dtype = BF16
    else:
        out_dtype = F32
    scratch.append(pltpu.VMEM((SUBLANES, cb), F32))
    return pl.pallas_call(
        functools.partial(_rglru_kernel, ns=ns, ts=ts, cb=cb, reverse=reverse),
        grid=(B, C // cb, ns),
        in_specs=in_specs,
        out_specs=cur,
        out_shape=jax.ShapeDtypeStruct((B, S, C), out_dtype),
        scratch_shapes=scratch,
        compiler_params=_cparams(("parallel", "parallel", "arbitrary")),
    )(*args)


def _lane_table(vals, jidx, mask, fill):
    return jnp.where(mask[None, :], vals[:, jidx], fill)


def _rope_tables(S, half, period, start, scale):
    inv = jnp.power(ROPE_THETA, -jnp.arange(half, dtype=F32) / half)
    ang = jnp.arange(S, dtype=F32)[:, None] * inv[None, :]
    cos, sin = jnp.cos(ang), jnp.sin(ang)
    e = np.arange(LANES) % period - start
    first = (e >= 0) & (e < half)
    second = (e >= half) & (e < 2 * half)
    jidx = np.where(first | second, e % half, 0)
    c = _lane_table(cos, jidx, first | second, 1.0)
    s1 = _lane_table(-sin, jidx, first, 0.0)
    s2 = _lane_table(sin, jidx, second, 0.0)
    return jnp.stack([c, s1, s2]) * scale


def _all_tables(S):
    mla_scale = float((MLA_NOPE + MLA_ROPE) ** -0.5 * np.log2(np.e))
    dil_scale = float(HEAD_DIM ** -0.5)
    return jnp.concatenate([
        _rope_tables(S, MLA_ROPE // 2, MLA_PAD, MLA_NOPE, mla_scale),
        _rope_tables(S, MLA_ROPE // 2, MLA_PAD, MLA_NOPE, 1.0),
        _rope_tables(S, DIL_ROT // 2, HEAD_DIM, 0, dil_scale),
        _rope_tables(S, DIL_ROT // 2, HEAD_DIM, 0, 1.0),
    ])


def _prep_weights(norm_mix, w_in_a, q_norm, w_uq, kv_norm, w_ukv, w_out_a, w_in_r, conv_w, conv_b,
                  lru_w_gate, lru_b_gate, lru_lambda, w_out_r, norm_ffn, w_gu, w_down, norm_final):
    w_in = w_in_a[0]
    qk_dim = MLA_NOPE + MLA_ROPE
    kr_cols = jnp.pad(w_in[:, MLA_Q_RANK + MLA_KV_RANK:MLA_IN],
                      ((0, 0), (MLA_NOPE, MLA_PAD - qk_dim)))
    w_mla = jnp.concatenate([w_in[:, :MLA_Q_RANK + MLA_KV_RANK], kr_cols], axis=1)
    uq = w_uq[0].reshape(MLA_Q_RANK, MLA_HEADS, qk_dim)
    uq = jnp.pad(uq, ((0, 0), (0, 0), (0, MLA_PAD - qk_dim))).reshape(MLA_Q_RANK, MLA_HEADS * MLA_PAD)
    ukv = w_ukv[0].reshape(MLA_KV_RANK, MLA_HEADS, MLA_NOPE + MLA_V)
    uk = jnp.pad(ukv[:, :, :MLA_NOPE], ((0, 0), (0, 0), (0, MLA_PAD - MLA_NOPE)))
    uk = uk.reshape(MLA_KV_RANK, MLA_HEADS * MLA_PAD)
    uv = ukv[:, :, MLA_NOPE:].reshape(MLA_KV_RANK, MLA_HEADS * MLA_V).T
    wg = lru_w_gate[0]
    w_gate = jnp.concatenate([wg[:, 0], wg[:, 1]], axis=-1)
    row = lambda v: v.reshape(1, -1)
    return {
        "norm_mix": [row(norm_mix[l]) for l in range(2)],
        "w_mla": w_mla.astype(BF16),
        "w_dil": w_in[:, MLA_IN:].astype(BF16),
        "q_norm": row(q_norm[0]),
        "w_uq": uq.astype(BF16),
        "kv_norm": row(kv_norm[0]),
        "w_uk": uk.astype(BF16),
        "w_uv": uv.astype(BF16),
        "w_out_a": w_out_a[0].astype(BF16),
        "w_in_r": w_in_r[0].astype(BF16),
        "conv_w": conv_w[0],
        "conv_b": row(conv_b[0]),
        "w_gate": w_gate.astype(BF16),
        "b_r": [row(lru_b_gate[0, d, 0]) for d in range(2)],
        "b_i": [row(lru_b_gate[0, d, 1]) for d in range(2)],
        "lam": [row(lru_lambda[0, d]) for d in range(2)],
        "w_out_r": w_out_r[0].astype(BF16),
        "norm_ffn": [row(norm_ffn[l]) for l in range(2)],
        "w_gu": [w_gu[l].astype(BF16) for l in range(2)],
        "w_down": [w_down[l].astype(BF16) for l in range(2)],
        "norm_final": row(norm_final),
    }


def _trunk(x, wp):
    B, S, D = x.shape
    T = B * S
    tabs = _all_tables(S)
    q, k, vt, *dgs = _l0_prep(x, wp["norm_mix"][0], wp, tabs, tm=256)
    o_mla = _mla_attention(q, k, vt, tq=min(S, 2048), tk=min(S, 2048))
    os_, ls_ = [], []
    for dg, (_, dil) in zip(dgs, DIL_PATTERNS):
        o_g, l_g = _dil_attention(dg, dil, tt=128)
        os_.append(o_g.reshape(T // dil, dil * DIL_W))
        ls_.append(l_g.reshape(T // dil, dil * DIL_W))
    x1 = _out_ffn_attn(x.reshape(T, D), o_mla.reshape(T, MLA_HEADS * MLA_V), os_, ls_, wp, 0, tm=256)
    y, xr = _l1_prep(x1, wp["norm_mix"][1], wp["w_in_r"], tm=256)
    xr = xr.reshape(B, S, D_RNN)
    hf = _rglru(xr, wp, ts=256, cb=512, reverse=False)
    a = _rglru(xr, wp, ts=256, cb=512, reverse=True, hf=hf, y=y.reshape(B, S, D_RNN))
    out = _out_ffn_rec(x1, a.reshape(T, D_RNN), wp, 1, tm=256)
    return out.reshape(B, S, D)


def kernel(x_prompt, x_sample, norm_mix, w_in_a, q_norm, w_uq, kv_norm, w_ukv, w_out_a, w_in_r, conv_w,
           conv_b, lru_w_gate, lru_b_gate, lru_lambda, w_out_r, norm_ffn, w_gu, w_down, norm_final):
    wp = _prep_weights(norm_mix, w_in_a, q_norm, w_uq, kv_norm, w_ukv, w_out_a, w_in_r, conv_w, conv_b,
                       lru_w_gate, lru_b_gate, lru_lambda, w_out_r, norm_ffn, w_gu, w_down, norm_final)
    return (_trunk(x_prompt, wp), _trunk(x_sample, wp))
```

```python
import functools

import numpy as np
import jax
import jax.numpy as jnp
from jax import lax
from jax.experimental import pallas as pl
from jax.experimental.pallas import tpu as pltpu

F32 = jnp.float32
BF16 = jnp.bfloat16

D_MODEL = 1024
HEAD_DIM = 64
ROPE_THETA = 500000.0
NORM_EPS = 1e-6
MLA_HEADS = 8
MLA_NOPE = 64
MLA_ROPE = 32
MLA_V = 64
MLA_Q_RANK = 256
MLA_KV_RANK = 128
DIL_PATTERNS = ((128, 1), (512, 4), (2048, 16))
DIL_GROUPS = len(DIL_PATTERNS)
DIL_HEADS = 8
DIL_ROT = HEAD_DIM // 4
DIL_STEPS = 64
DIL_QB = 128
MLA_IN = MLA_Q_RANK + MLA_KV_RANK + MLA_ROPE
DIL_W = DIL_HEADS * HEAD_DIM
DIL_QKV = 3 * DIL_GROUPS * DIL_W
D_RNN = 1536
LRU_BLOCKS = 12
LRU_BW = D_RNN // LRU_BLOCKS
CONV_W = 4
CONV_LEFT = 2
LRU_C = 8.0
D_FF = ((8 * D_MODEL // 3 + 255) // 256) * 256
NEG_BIG = -1e30

LANES = 128
SUBLANES = 8
MLA_PAD = 128
VMEM_LIMIT = 56 * 1024 * 1024

FF_CHUNKS = 2
FF_CHUNK = D_FF // FF_CHUNKS


def _cparams(sem):
    return pltpu.CompilerParams(dimension_semantics=sem, vmem_limit_bytes=VMEM_LIMIT)


def _const_spec(shape):
    nd = len(shape)
    return pl.BlockSpec(shape, lambda *_: (0,) * nd, pipeline_mode=pl.Buffered(1))


def _rms(x, g):
    return x * lax.rsqrt(jnp.mean(x * x, axis=-1, keepdims=True) + NORM_EPS) * g


def _rope_block(xb, c, s1, s2, half):
    return xb * c + pltpu.roll(xb, LANES - half, 1) * s1 + pltpu.roll(xb, half, 1) * s2


def _l0_prep_kernel(x_ref, g_ref, wmla_ref, wdil_ref, qn_ref, wuq_ref, kvn_ref, wuk_ref, wuv_ref,
                    tab_ref, q_out, k_out, v_out, d0_out, d1_out, d2_out, perm_scr):
    tm = x_ref.shape[1]
    d_outs = (d0_out, d1_out, d2_out)
    x = x_ref[0]
    h = _rms(x, g_ref[...]).astype(BF16)
    z = jnp.dot(h, wmla_ref[...], preferred_element_type=F32)
    qn = _rms(z[:, :MLA_Q_RANK], qn_ref[...]).astype(BF16)
    kvn = _rms(z[:, MLA_Q_RANK:MLA_Q_RANK + MLA_KV_RANK], kvn_ref[...]).astype(BF16)
    kr = z[:, MLA_Q_RANK + MLA_KV_RANK:]
    qf = jnp.dot(qn, wuq_ref[...], preferred_element_type=F32)
    kf = jnp.dot(kvn, wuk_ref[...], preferred_element_type=F32)
    vt = lax.dot_general(wuv_ref[...], kvn, (((1,), (1,)), ((), ())),
                         preferred_element_type=F32)
    half = MLA_ROPE // 2
    krr = _rope_block(kr, tab_ref[3], tab_ref[4], tab_ref[5], half)
    for hh in range(MLA_HEADS):
        sl = slice(hh * MLA_PAD, (hh + 1) * MLA_PAD)
        q_out[0, :, sl] = _rope_block(qf[:, sl], tab_ref[0], tab_ref[1], tab_ref[2], half).astype(BF16)
        k_out[0, :, sl] = (kf[:, sl] + krr).astype(BF16)
    v_out[0] = vt.astype(BF16)
    dhalf = DIL_ROT // 2
    for j in range(3 * DIL_GROUPS):
        c, g = divmod(j, DIL_GROUPS)
        dil = DIL_PATTERNS[g][1]
        zc = jnp.dot(h, wdil_ref[:, j * DIL_W:(j + 1) * DIL_W], preferred_element_type=F32)
        for b in range(DIL_W // LANES):
            zb = zc[:, b * LANES:(b + 1) * LANES]
            if c < 2:
                t0 = 6 + 3 * c
                zb = _rope_block(zb, tab_ref[t0], tab_ref[t0 + 1], tab_ref[t0 + 2], dhalf)
            if dil == 1:
                d_outs[g][0, :, c * DIL_W + b * LANES:c * DIL_W + (b + 1) * LANES] = zb.astype(BF16)
            else:
                perm_scr[j, b] = zb
        if dil > 1:
            for r in range(dil):
                for b in range(DIL_W // LANES):
                    col = (r * 3 + c) * DIL_W + b * LANES
                    d_outs[g][0, :, col:col + LANES] = perm_scr[
                        j, b, pl.ds(r, tm // dil, stride=dil), :].astype(BF16)


def _l0_prep(x, g, wp, tabs, tm):
    B, S, D = x.shape
    grid = (B, S // tm)
    row = lambda w: pl.BlockSpec((1, tm, w), lambda b, i: (b, i, 0))
    return pl.pallas_call(
        _l0_prep_kernel,
        grid=grid,
        in_specs=[
            row(D),
            _const_spec((1, D)),
            _const_spec(wp["w_mla"].shape),
            _const_spec(wp["w_dil"].shape),
            _const_spec((1, MLA_Q_RANK)),
            _const_spec(wp["w_uq"].shape),
            _const_spec((1, MLA_KV_RANK)),
            _const_spec(wp["w_uk"].shape),
            _const_spec(wp["w_uv"].shape),
            pl.BlockSpec((12, tm, LANES), lambda b, i: (0, i, 0)),
        ],
        out_specs=[row(MLA_HEADS * MLA_PAD), row(MLA_HEADS * MLA_PAD),
                   pl.BlockSpec((1, MLA_HEADS * MLA_V, tm), lambda b, i: (b, 0, i))] + [
            pl.BlockSpec((1, tm // d, d * 3 * DIL_W), lambda b, i: (b, i, 0)) for _, d in DIL_PATTERNS],
        out_shape=[
            jax.ShapeDtypeStruct((B, S, MLA_HEADS * MLA_PAD), BF16),
            jax.ShapeDtypeStruct((B, S, MLA_HEADS * MLA_PAD), BF16),
            jax.ShapeDtypeStruct((B, MLA_HEADS * MLA_V, S), BF16),
        ] + [jax.ShapeDtypeStruct((B, S // d, d * 3 * DIL_W), BF16) for _, d in DIL_PATTERNS],
        scratch_shapes=[pltpu.VMEM((3 * DIL_GROUPS, DIL_W // LANES, tm, LANES), F32)],
        compiler_params=_cparams(("parallel", "parallel")),
    )(x, g, wp["w_mla"], wp["w_dil"], wp["q_norm"], wp["w_uq"], wp["kv_norm"], wp["w_uk"], wp["w_uv"], tabs)


def _mla_kernel(q_ref, k_ref, vt_ref, o_ref, m_scr, acc_scr, *, nk, cq):
    ki = pl.program_id(3)

    @pl.when(ki == 0)
    def _():
        m_scr[...] = jnp.full(m_scr.shape, -jnp.inf, F32)
        acc_scr[...] = jnp.zeros(acc_scr.shape, F32)

    tq = q_ref.shape[1]
    vt = vt_ref[0]
    own_rows = lax.broadcasted_iota(jnp.int32, vt.shape, 0) < MLA_V
    ones = jnp.ones(vt.shape, BF16)
    v1 = [jnp.where(own_rows, vt, ones), jnp.where(own_rows, ones, vt)]

    def scores(hh, c):
        q = q_ref[0, c * cq:(c + 1) * cq, hh * MLA_PAD:(hh + 1) * MLA_PAD]
        k = k_ref[0, :, hh * MLA_PAD:(hh + 1) * MLA_PAD]
        st = lax.dot_general(k, q, (((1,), (1,)), ((), ())), preferred_element_type=F32)
        m_prev = m_scr[hh, :, c * cq:(c + 1) * cq]
        m_new = jnp.maximum(m_prev, jnp.max(st, axis=0, keepdims=True))
        m_scr[hh, :, c * cq:(c + 1) * cq] = m_new
        return st, m_prev, m_new

    def accumulate(hh, c, st, m_prev, m_new):
        alpha = jnp.exp2(m_prev - m_new)
        p = jnp.exp2(st - m_new).astype(BF16)
        acc_scr[hh, :, c * cq:(c + 1) * cq] = alpha * acc_scr[hh, :, c * cq:(c + 1) * cq] + jnp.dot(
            v1[hh], p, preferred_element_type=F32)

    tasks = [(hh, c) for hh in range(2) for c in range(tq // cq)]
    pending = scores(*tasks[0])
    for t, task in enumerate(tasks):
        nxt = scores(*tasks[t + 1]) if t + 1 < len(tasks) else None
        accumulate(*task, *pending)
        pending = nxt

    @pl.when(ki == nk - 1)
    def _():
        a0, a1 = acc_scr[0], acc_scr[1]
        ot = jnp.concatenate([a0[:MLA_V] / a0[MLA_V:MLA_V + 1], a1[MLA_V:] / a1[0:1]], axis=0)
        o_ref[0] = ot.T.astype(BF16)


def _mla_attention(q, k, vt, tq, tk, cq=1024):
    B, S, _ = q.shape
    nq, nk = S // tq, S // tk
    hp = MLA_HEADS // 2
    return pl.pallas_call(
        functools.partial(_mla_kernel, nk=nk, cq=cq),
        grid=(B, hp, nq, nk),
        in_specs=[
            pl.BlockSpec((1, tq, 2 * MLA_PAD), lambda b, h, i, j: (b, i, h)),
            pl.BlockSpec((1, tk, 2 * MLA_PAD), lambda b, h, i, j: (b, j, h)),
            pl.BlockSpec((1, 2 * MLA_V, tk), lambda b, h, i, j: (b, h, j)),
        ],
        out_specs=pl.BlockSpec((1, tq, 2 * MLA_V), lambda b, h, i, j: (b, i, h)),
        out_shape=jax.ShapeDtypeStruct((B, S, MLA_HEADS * MLA_V), BF16),
        scratch_shapes=[
            pltpu.VMEM((2, 1, tq), F32),
            pltpu.VMEM((2, 2 * MLA_V, tq), F32),
        ],
        compiler_params=_cparams(("parallel", "parallel", "parallel", "arbitrary")),
    )(q, k, vt)


def _dil_kernel(q_ref, kp_ref, kc_ref, kn_ref, vp_ref, vc_ref, vn_ref, o_ref, lse_ref, *, nt, tt):
    i = pl.program_id(2)
    kext = jnp.concatenate([kp_ref[0], kc_ref[0], kn_ref[0]], axis=0)
    vext = jnp.concatenate([vp_ref[0], vc_ref[0], vn_ref[0]], axis=0)
    nsub = tt // DIL_QB
    nkx = DIL_QB + 2 * DIL_STEPS
    qi = lax.broadcasted_iota(jnp.int32, (DIL_QB, nkx), 0)
    kx = lax.broadcasted_iota(jnp.int32, (DIL_QB, nkx), 1)
    rel = kx - DIL_STEPS - qi
    band = (rel >= -DIL_STEPS) & (rel <= DIL_STEPS)
    lane_q = lax.broadcasted_iota(jnp.int32, (DIL_QB, LANES), 1)
    first = lane_q < HEAD_DIM
    zero = jnp.zeros((DIL_QB, LANES), BF16)
    for j in range(nsub):
        ok = band
        if j == 0:
            ok = ok & ((kx >= DIL_STEPS) | (i > 0))
        if j == nsub - 1:
            ok = ok & ((kx < DIL_QB + DIL_STEPS) | (i < nt - 1))
        rows = slice(j * DIL_QB, (j + 1) * DIL_QB)
        krows = slice(j * DIL_QB, j * DIL_QB + nkx)
        for hp in range(DIL_HEADS // 2):
            sl = slice(hp * LANES, (hp + 1) * LANES)
            qp = q_ref[0, rows, sl]
            kp = kext[krows, sl]
            vp = vext[krows, sl]
            outs, lses = [], []
            for hh in range(2):
                qh = jnp.where(first if hh == 0 else ~first, qp, zero)
                s = lax.dot_general(qh, kp, (((1,), (1,)), ((), ())), preferred_element_type=F32)
                s = jnp.where(ok, s, NEG_BIG)
                m = jnp.max(s, axis=1, keepdims=True)
                e = jnp.exp2(s - m)
                den = jnp.sum(e, axis=1, keepdims=True)
                outs.append(jnp.dot(e.astype(BF16), vp, preferred_element_type=F32) / den)
                lses.append(jnp.broadcast_to(m + jnp.log2(den), (DIL_QB, LANES)))
            o_ref[0, rows, sl] = jnp.where(first, outs[0], outs[1])
            lse_ref[0, rows, sl] = jnp.where(first, lses[0], lses[1])


def _dil_attention(x, dil, tt):
    B, L, _ = x.shape
    nt = L // tt
    nblk = 3
    hb = tt // DIL_STEPS
    nhb = L // DIL_STEPS
    cur = lambda c: pl.BlockSpec((1, tt, DIL_W), lambda b, r, i: (b, i, r * nblk + c))
    prev = lambda c: pl.BlockSpec((1, DIL_STEPS, DIL_W),
                                  lambda b, r, i: (b, jnp.maximum(i * hb - 1, 0), r * nblk + c))
    nxt = lambda c: pl.BlockSpec((1, DIL_STEPS, DIL_W),
                                 lambda b, r, i: (b, jnp.minimum((i + 1) * hb, nhb - 1), r * nblk + c))
    out_spec = pl.BlockSpec((1, tt, DIL_W), lambda b, r, i: (b, i, r))
    return pl.pallas_call(
        functools.partial(_dil_kernel, nt=nt, tt=tt),
        grid=(B, dil, nt),
        in_specs=[cur(0), prev(1), cur(1), nxt(1), prev(2), cur(2), nxt(2)],
        out_specs=[out_spec, out_spec],
        out_shape=[jax.ShapeDtypeStruct((B, L, dil * DIL_W), F32)] * 2,
        compiler_params=_cparams(("parallel", "parallel", "parallel")),
    )(x, x, x, x, x, x, x)


def _ffn(x1, gf, wgu_ref, wdown_ref):
    h2 = _rms(x1, gf).astype(BF16)
    acc = x1
    for c in range(FF_CHUNKS):
        gate = jnp.dot(h2, wgu_ref[:, c * FF_CHUNK:(c + 1) * FF_CHUNK], preferred_element_type=F32)
        up = jnp.dot(h2, wgu_ref[:, D_FF + c * FF_CHUNK:D_FF + (c + 1) * FF_CHUNK],
                     preferred_element_type=F32)
        act = (gate * jax.nn.sigmoid(gate) * up).astype(BF16)
        acc = acc + jnp.dot(act, wdown_ref[c * FF_CHUNK:(c + 1) * FF_CHUNK, :], preferred_element_type=F32)
    return acc


def _out_ffn_attn_kernel(x_ref, om_ref, o0_ref, o1_ref, o2_ref, l0_ref, l1_ref, l2_ref,
                         wo_ref, gf_ref, wgu_ref, wdown_ref, y_ref, perm_scr):
    tm = x_ref.shape[0]

    def natural(ref, slot, dil):
        for r in range(dil):
            for b in range(DIL_W // LANES):
                perm_scr[slot, b, pl.ds(r, tm // dil, stride=dil), :] = ref[
                    :, r * DIL_W + b * LANES:r * DIL_W + (b + 1) * LANES]
        return jnp.concatenate([perm_scr[slot, b] for b in range(DIL_W // LANES)], axis=1)

    d1, d2 = DIL_PATTERNS[1][1], DIL_PATTERNS[2][1]
    l0, l1, l2 = l0_ref[...], natural(l1_ref, 0, d1), natural(l2_ref, 1, d2)
    mx = jnp.maximum(jnp.maximum(l0, l1), l2)
    e0, e1, e2 = jnp.exp2(l0 - mx), jnp.exp2(l1 - mx), jnp.exp2(l2 - mx)
    od = (e0 * o0_ref[...] + e1 * natural(o1_ref, 2, d1) + e2 * natural(o2_ref, 3, d2)) / (e0 + e1 + e2)
    o = jnp.concatenate([om_ref[...], od.astype(BF16)], axis=1)
    x1 = x_ref[...] + jnp.dot(o, wo_ref[...], preferred_element_type=F32)
    y_ref[...] = _ffn(x1, gf_ref[...], wgu_ref, wdown_ref)


def _out_ffn_rec_kernel(x_ref, a_ref, wo_ref, gf_ref, wgu_ref, wdown_ref, gfin_ref, y_ref):
    x1 = x_ref[...] + jnp.dot(a_ref[...], wo_ref[...], preferred_element_type=F32)
    y_ref[...] = _rms(_ffn(x1, gf_ref[...], wgu_ref, wdown_ref), gfin_ref[...])


def _out_ffn_attn(x, om, os_, ls_, wp, layer, tm):
    T, D = x.shape
    row = lambda w: pl.BlockSpec((tm, w), lambda i: (i, 0))
    grp = [pl.BlockSpec((tm // d, d * DIL_W), lambda i: (i, 0)) for _, d in DIL_PATTERNS]
    return pl.pallas_call(
        _out_ffn_attn_kernel,
        grid=(T // tm,),
        in_specs=[row(D), row(DIL_W)] + grp + grp + [
            _const_spec((MLA_HEADS * MLA_V + DIL_W, D)), _const_spec((1, D)),
            _const_spec((D, 2 * D_FF)), _const_spec((D_FF, D))],
        out_specs=row(D),
        out_shape=jax.ShapeDtypeStruct((T, D), F32),
        scratch_shapes=[pltpu.VMEM((4, DIL_W // LANES, tm, LANES), F32)],
        compiler_params=_cparams(("parallel",)),
    )(x, om, *os_, *ls_, wp["w_out_a"], wp["norm_ffn"][layer],
      wp["w_gu"][layer], wp["w_down"][layer])


def _out_ffn_rec(x, a, wp, layer, tm):
    T, D = x.shape
    row = lambda w: pl.BlockSpec((tm, w), lambda i: (i, 0))
    return pl.pallas_call(
        _out_ffn_rec_kernel,
        grid=(T // tm,),
        in_specs=[row(D), row(D_RNN), _const_spec((D_RNN, D)), _const_spec((1, D)),
                  _const_spec((D, 2 * D_FF)), _const_spec((D_FF, D)), _const_spec((1, D))],
        out_specs=row(D),
        out_shape=jax.ShapeDtypeStruct((T, D), F32),
        compiler_params=_cparams(("parallel",)),
    )(x, a, wp["w_out_r"], wp["norm_ffn"][layer], wp["w_gu"][layer], wp["w_down"][layer],
      wp["norm_final"])


def _l1_prep_kernel(x_ref, g_ref, w_ref, y_ref, xr_ref):
    h = _rms(x_ref[...], g_ref[...]).astype(BF16)
    zy = jnp.dot(h, w_ref[:, :D_RNN], preferred_element_type=F32)
    y_ref[...] = (0.5 * zy * (1.0 + jnp.tanh(np.sqrt(2.0 / np.pi).astype(np.float32)
                                             * (zy + 0.044715 * (zy * zy * zy))))).astype(BF16)
    xr_ref[...] = jnp.dot(h, w_ref[:, D_RNN:], preferred_element_type=F32)


def _l1_prep(x, g, w, tm):
    T, D = x.shape
    row = lambda w_: pl.BlockSpec((tm, w_), lambda i: (i, 0))
    return pl.pallas_call(
        _l1_prep_kernel,
        grid=(T // tm,),
        in_specs=[row(D), _const_spec((1, D)), _const_spec((D, 2 * D_RNN))],
        out_specs=[row(D_RNN), row(D_RNN)],
        out_shape=[jax.ShapeDtypeStruct((T, D_RNN), BF16), jax.ShapeDtypeStruct((T, D_RNN), F32)],
        compiler_params=_cparams(("parallel",)),
    )(x, g, w)


def _rglru_kernel(*refs, ns, ts, cb, reverse):
    if reverse:
        (xp_ref, xc_ref, xn_ref, cw_ref, cbias_ref, wg_ref, br_ref, bi_ref, lam_ref, hf_ref, y_ref,
         o_ref, a_scr, u_scr, h_scr, carry_scr) = refs
    else:
        (xp_ref, xc_ref, xn_ref, cw_ref, cbias_ref, wg_ref, br_ref, bi_ref, lam_ref,
         o_ref, a_scr, u_scr, carry_scr) = refs
        h_scr = o_ref.at[0]
    step = pl.program_id(2)
    tile = (ns - 1 - step) if reverse else step

    @pl.when(step == 0)
    def _():
        carry_scr[...] = jnp.zeros(carry_scr.shape, F32)

    x0 = xc_ref[0]
    rows = lax.broadcasted_iota(jnp.int32, (ts, cb), 0)
    pv = jnp.where(tile > 0, xp_ref[0], 0.0)
    nv = jnp.where(tile < ns - 1, xn_ref[0], 0.0)
    xm1 = jnp.where(rows == 0, pv[SUBLANES - 1:SUBLANES], pltpu.roll(x0, 1, 0))
    xm2 = pltpu.roll(x0, 2, 0)
    xm2 = jnp.where(rows == 0, pv[SUBLANES - 2:SUBLANES - 1], xm2)
    xm2 = jnp.where(rows == 1, pv[SUBLANES - 1:SUBLANES], xm2)
    xp1 = jnp.where(rows == ts - 1, nv[0:1], pltpu.roll(x0, ts - 1, 0))
    xc = (xm2 * cw_ref[0:1] + xm1 * cw_ref[1:2] + x0 * cw_ref[2:3] + xp1 * cw_ref[3:4]) + cbias_ref[...]

    xcb = xc.astype(BF16)
    for j in range(cb // LRU_BW):
        sl = slice(j * LRU_BW, (j + 1) * LRU_BW)
        gts = jnp.dot(xcb[:, sl], wg_ref[j], preferred_element_type=F32)
        r = jax.nn.sigmoid(gts[:, :LRU_BW] + br_ref[:, sl])
        ig = jax.nn.sigmoid(gts[:, LRU_BW:] + bi_ref[:, sl])
        nlam = -lam_ref[:, sl]
        softplus = jnp.maximum(nlam, 0.0) + jnp.log1p(jnp.exp(-jnp.abs(nlam)))
        a = jnp.exp((-LRU_C) * r * softplus)
        a_scr[:, sl] = a
        u_scr[:, sl] = jnp.sqrt(jnp.maximum(1.0 - a * a, 0.0)) * (ig * xc[:, sl])

    row8 = lax.broadcasted_iota(jnp.int32, (SUBLANES, cb), 0)
    nchunk = ts // SUBLANES

    def chunk(c, hprev):
        idx = (nchunk - 1 - c) if reverse else c
        r0 = pl.multiple_of(idx * SUBLANES, SUBLANES)
        a = a_scr[pl.ds(r0, SUBLANES), :]
        b = u_scr[pl.ds(r0, SUBLANES), :]
        for d in (1, 2, 4):
            if reverse:
                keep = row8 < SUBLANES - d
                sh = SUBLANES - d
            else:
                keep = row8 >= d
                sh = d
            a_s = jnp.where(keep, pltpu.roll(a, sh, 0), 1.0)
            b_s = jnp.where(keep, pltpu.roll(b, sh, 0), 0.0)
            b = a * b_s + b
            a = a * a_s
        h = a * hprev + b
        h_scr[pl.ds(r0, SUBLANES), :] = h
        edge = h[0:1] if reverse else h[SUBLANES - 1:SUBLANES]
        return jnp.broadcast_to(edge, (SUBLANES, cb))

    carry_scr[...] = lax.fori_loop(0, nchunk, chunk, carry_scr[...], unroll=4)

    if reverse:
        o_ref[0] = ((hf_ref[0] + h_scr[...]) * y_ref[0].astype(F32)).astype(BF16)


def _rglru(xr, wp, ts, cb, reverse, hf=None, y=None):
    B, S, C = xr.shape
    ns = S // ts
    nhb = S // SUBLANES
    hb = ts // SUBLANES
    d = 1 if reverse else 0
    tidx = (lambda i: ns - 1 - i) if reverse else (lambda i: i)
    cur = pl.BlockSpec((1, ts, cb), lambda b, c, i: (b, tidx(i), c))
    prev = pl.BlockSpec((1, SUBLANES, cb), lambda b, c, i: (b, jnp.maximum(tidx(i) * hb - 1, 0), c))
    nxt = pl.BlockSpec((1, SUBLANES, cb), lambda b, c, i: (b, jnp.minimum((tidx(i) + 1) * hb, nhb - 1), c))
    vec = pl.BlockSpec((1, cb), lambda b, c, i: (0, c))
    in_specs = [prev, cur, nxt,
                pl.BlockSpec((CONV_W, cb), lambda b, c, i: (0, c)), vec,
                pl.BlockSpec((cb // LRU_BW, LRU_BW, 2 * LRU_BW), lambda b, c, i: (c, 0, 0)),
                vec, vec, vec]
    args = [xr, xr, xr, wp["conv_w"], wp["conv_b"], wp["w_gate"][d], wp["b_r"][d], wp["b_i"][d],
            wp["lam"][d]]
    scratch = [pltpu.VMEM((ts, cb), F32), pltpu.VMEM((ts, cb), F32)]
    if reverse:
        in_specs += [cur, cur]
        args += [hf, y]
        scratch.append(pltpu.VMEM((ts, cb), F32))
        out_dtype = BF16
    else:
        out_dtype = F32
    scratch.append(pltpu.VMEM((SUBLANES, cb), F32))
    return pl.pallas_call(
        functools.partial(_rglru_kernel, ns=ns, ts=ts, cb=cb, reverse=reverse),
        grid=(B, C // cb, ns),
        in_specs=in_specs,
        out_specs=cur,
        out_shape=jax.ShapeDtypeStruct((B, S, C), out_dtype),
        scratch_shapes=scratch,
        compiler_params=_cparams(("parallel", "parallel", "arbitrary")),
    )(*args)


def _lane_table(vals, jidx, mask, fill):
    return jnp.where(mask[None, :], vals[:, jidx], fill)


def _rope_tables(S, half, period, start, scale):
    inv = jnp.power(ROPE_THETA, -jnp.arange(half, dtype=F32) / half)
    ang = jnp.arange(S, dtype=F32)[:, None] * inv[None, :]
    cos, sin = jnp.cos(ang), jnp.sin(ang)
    e = np.arange(LANES) % period - start
    first = (e >= 0) & (e < half)
    second = (e >= half) & (e < 2 * half)
    jidx = np.where(first | second, e % half, 0)
    c = _lane_table(cos, jidx, first | second, 1.0)
    s1 = _lane_table(-sin, jidx, first, 0.0)
    s2 = _lane_table(sin, jidx, second, 0.0)
    return jnp.stack([c, s1, s2]) * scale


def _all_tables(S):
    mla_scale = float((MLA_NOPE + MLA_ROPE) ** -0.5 * np.log2(np.e))
    dil_scale = float(HEAD_DIM ** -0.5 * np.log2(np.e))
    return jnp.concatenate([
        _rope_tables(S, MLA_ROPE // 2, MLA_PAD, MLA_NOPE, mla_scale),
        _rope_tables(S, MLA_ROPE // 2, MLA_PAD, MLA_NOPE, 1.0),
        _rope_tables(S, DIL_ROT // 2, HEAD_DIM, 0, dil_scale),
        _rope_tables(S, DIL_ROT // 2, HEAD_DIM, 0, 1.0),
    ])


def _prep_weights(norm_mix, w_in_a, q_norm, w_uq, kv_norm, w_ukv, w_out_a, w_in_r, conv_w, conv_b,
                  lru_w_gate, lru_b_gate, lru_lambda, w_out_r, norm_ffn, w_gu, w_down, norm_final):
    w_in = w_in_a[0]
    qk_dim = MLA_NOPE + MLA_ROPE
    kr_cols = jnp.pad(w_in[:, MLA_Q_RANK + MLA_KV_RANK:MLA_IN],
                      ((0, 0), (MLA_NOPE, MLA_PAD - qk_dim)))
    w_mla = jnp.concatenate([w_in[:, :MLA_Q_RANK + MLA_KV_RANK], kr_cols], axis=1)
    uq = w_uq[0].reshape(MLA_Q_RANK, MLA_HEADS, qk_dim)
    uq = jnp.pad(uq, ((0, 0), (0, 0), (0, MLA_PAD - qk_dim))).reshape(MLA_Q_RANK, MLA_HEADS * MLA_PAD)
    ukv = w_ukv[0].reshape(MLA_KV_RANK, MLA_HEADS, MLA_NOPE + MLA_V)
    uk = jnp.pad(ukv[:, :, :MLA_NOPE], ((0, 0), (0, 0), (0, MLA_PAD - MLA_NOPE)))
    uk = uk.reshape(MLA_KV_RANK, MLA_HEADS * MLA_PAD)
    uv = ukv[:, :, MLA_NOPE:].reshape(MLA_KV_RANK, MLA_HEADS * MLA_V).T
    wg = lru_w_gate[0]
    w_gate = jnp.concatenate([wg[:, 0], wg[:, 1]], axis=-1)
    row = lambda v: v.reshape(1, -1)
    return {
        "norm_mix": [row(norm_mix[l]) for l in range(2)],
        "w_mla": w_mla.astype(BF16),
        "w_dil": w_in[:, MLA_IN:].astype(BF16),
        "q_norm": row(q_norm[0]),
        "w_uq": uq.astype(BF16),
        "kv_norm": row(kv_norm[0]),
        "w_uk": uk.astype(BF16),
        "w_uv": uv.astype(BF16),
        "w_out_a": w_out_a[0].astype(BF16),
        "w_in_r": w_in_r[0].astype(BF16),
        "conv_w": conv_w[0],
        "conv_b": row(conv_b[0]),
        "w_gate": w_gate.astype(BF16),
        "b_r": [row(lru_b_gate[0, d, 0]) for d in range(2)],
        "b_i": [row(lru_b_gate[0, d, 1]) for d in range(2)],
        "lam": [row(lru_lambda[0, d]) for d in range(2)],
        "w_out_r": w_out_r[0].astype(BF16),
        "norm_ffn": [row(norm_ffn[l]) for l in range(2)],
        "w_gu": [w_gu[l].astype(BF16) for l in range(2)],
        "w_down": [w_down[l].astype(BF16) for l in range(2)],
        "norm_final": row(norm_final),
    }


def _trunk(x, wp):
    B, S, D = x.shape
    T = B * S
    tabs = _all_tables(S)
    q, k, vt, *dgs = _l0_prep(x, wp["norm_mix"][0], wp, tabs, tm=256)
    o_mla = _mla_attention(q, k, vt, tq=min(S, 2048), tk=min(S, 2048))
    os_, ls_ = [], []
    for dg, (_, dil) in zip(dgs, DIL_PATTERNS):
        o_g, l_g = _dil_attention(dg, dil, tt=min(S // dil, 512))
        os_.append(o_g.reshape(T // dil, dil * DIL_W))
        ls_.append(l_g.reshape(T // dil, dil * DIL_W))
    x1 = _out_ffn_attn(x.reshape(T, D), o_mla.reshape(T, MLA_HEADS * MLA_V), os_, ls_, wp, 0, tm=256)
    y, xr = _l1_prep(x1, wp["norm_mix"][1], wp["w_in_r"], tm=256)
    xr = xr.reshape(B, S, D_RNN)
    hf = _rglru(xr, wp, ts=512, cb=512, reverse=False)
    a = _rglru(xr, wp, ts=512, cb=512, reverse=True, hf=hf, y=y.reshape(B, S, D_RNN))
    out = _out_ffn_rec(x1, a.reshape(T, D_RNN), wp, 1, tm=256)
    return out.reshape(B, S, D)


def kernel(x_prompt, x_sample, norm_mix, w_in_a, q_norm, w_uq, kv_norm, w_ukv, w_out_a, w_in_r, conv_w,
           conv_b, lru_w_gate, lru_b_gate, lru_lambda, w_out_r, norm_ffn, w_gu, w_down, norm_final):
    wp = _prep_weights(norm_mix, w_in_a, q_norm, w_uq, kv_norm, w_ukv, w_out_a, w_in_r, conv_w, conv_b,
                       lru_w_gate, lru_b_gate, lru_lambda, w_out_r, norm_ffn, w_gu, w_down, norm_final)
    return (_trunk(x_prompt, wp), _trunk(x_sample, wp))
```

```python
import functools

import numpy as np
import jax
import jax.numpy as jnp
from jax import lax
from jax.experimental import pallas as pl
from jax.experimental.pallas import tpu as pltpu

F32 = jnp.float32
BF16 = jnp.bfloat16

D_MODEL = 1024
HEAD_DIM = 64
ROPE_THETA = 500000.0
NORM_EPS = 1e-6
MLA_HEADS = 8
MLA_NOPE = 64
MLA_ROPE = 32
MLA_V = 64
MLA_Q_RANK = 256
MLA_KV_RANK = 128
DIL_PATTERNS = ((128, 1), (512, 4), (2048, 16))
DIL_GROUPS = len(DIL_PATTERNS)
DIL_HEADS = 8
DIL_ROT = HEAD_DIM // 4
DIL_STEPS = 64
DIL_QB = 128
DIL_ROWS_PER_STEP = 512
MLA_IN = MLA_Q_RANK + MLA_KV_RANK + MLA_ROPE
DIL_W = DIL_HEADS * HEAD_DIM
DIL_QKV = 3 * DIL_GROUPS * DIL_W
D_RNN = 1536
LRU_BLOCKS = 12
LRU_BW = D_RNN // LRU_BLOCKS
CONV_W = 4
CONV_LEFT = 2
LRU_C = 8.0
D_FF = ((8 * D_MODEL // 3 + 255) // 256) * 256
NEG_BIG = -1e30

LANES = 128
SUBLANES = 8
MLA_PAD = 128
VMEM_LIMIT = 56 * 1024 * 1024

FF_CHUNK = 256
FF_CHUNKS = D_FF // FF_CHUNK


def _cparams(sem):
    return pltpu.CompilerParams(dimension_semantics=sem, vmem_limit_bytes=VMEM_LIMIT)


def _const_spec(shape):
    nd = len(shape)
    return pl.BlockSpec(shape, lambda *_: (0,) * nd, pipeline_mode=pl.Buffered(1))


def _rms(x, g):
    return x * lax.rsqrt(jnp.mean(x * x, axis=-1, keepdims=True) + NORM_EPS) * g


def _rope_block(xb, c, s1, s2, half):
    return xb * c + pltpu.roll(xb, LANES - half, 1) * s1 + pltpu.roll(xb, half, 1) * s2


def _l0_prep_kernel(x_ref, g_ref, wmla_ref, wdil_ref, qn_ref, wuq_ref, kvn_ref, wuk_ref, wuv_ref,
                    tab_ref, q_out, k_out, v_out, d0_out, d1_out, d2_out, perm_scr):
    tm = x_ref.shape[1]
    d_outs = (d0_out, d1_out, d2_out)
    x = x_ref[0]
    h = _rms(x, g_ref[...]).astype(BF16)
    z = jnp.dot(h, wmla_ref[...], preferred_element_type=F32)
    qn = _rms(z[:, :MLA_Q_RANK], qn_ref[...]).astype(BF16)
    kvn = _rms(z[:, MLA_Q_RANK:MLA_Q_RANK + MLA_KV_RANK], kvn_ref[...]).astype(BF16)
    kr = z[:, MLA_Q_RANK + MLA_KV_RANK:]
    qf = jnp.dot(qn, wuq_ref[...], preferred_element_type=F32)
    kf = jnp.dot(kvn, wuk_ref[...], preferred_element_type=F32)
    vt = lax.dot_general(wuv_ref[...], kvn, (((1,), (1,)), ((), ())),
                         preferred_element_type=F32)
    half = MLA_ROPE // 2
    krr = _rope_block(kr, tab_ref[3], tab_ref[4], tab_ref[5], half)
    for hh in range(MLA_HEADS):
        sl = slice(hh * MLA_PAD, (hh + 1) * MLA_PAD)
        q_out[0, :, sl] = _rope_block(qf[:, sl], tab_ref[0], tab_ref[1], tab_ref[2], half).astype(BF16)
        k_out[0, :, sl] = (kf[:, sl] + krr).astype(BF16)
    v_out[0] = vt.astype(BF16)
    dhalf = DIL_ROT // 2
    for j in range(3 * DIL_GROUPS):
        c, g = divmod(j, DIL_GROUPS)
        dil = DIL_PATTERNS[g][1]
        zc = jnp.dot(h, wdil_ref[:, j * DIL_W:(j + 1) * DIL_W], preferred_element_type=F32)
        for b in range(DIL_W // LANES):
            zb = zc[:, b * LANES:(b + 1) * LANES]
            if c < 2:
                t0 = 6 + 3 * c
                zb = _rope_block(zb, tab_ref[t0], tab_ref[t0 + 1], tab_ref[t0 + 2], dhalf)
            if dil == 1:
                d_outs[g][0, :, c * DIL_W + b * LANES:c * DIL_W + (b + 1) * LANES] = zb.astype(BF16)
            else:
                perm_scr[b] = zb
        if dil > 1:
            for r in range(dil):
                for b in range(DIL_W // LANES):
                    col = (r * 3 + c) * DIL_W + b * LANES
                    d_outs[g][0, :, col:col + LANES] = perm_scr[
                        b, pl.ds(r, tm // dil, stride=dil), :].astype(BF16)


def _l0_prep(x, g, wp, tabs, tm):
    B, S, D = x.shape
    grid = (B, S // tm)
    row = lambda w: pl.BlockSpec((1, tm, w), lambda b, i: (b, i, 0))
    return pl.pallas_call(
        _l0_prep_kernel,
        grid=grid,
        in_specs=[
            row(D),
            _const_spec((1, D)),
            _const_spec(wp["w_mla"].shape),
            _const_spec(wp["w_dil"].shape),
            _const_spec((1, MLA_Q_RANK)),
            _const_spec(wp["w_uq"].shape),
            _const_spec((1, MLA_KV_RANK)),
            _const_spec(wp["w_uk"].shape),
            _const_spec(wp["w_uv"].shape),
            pl.BlockSpec((12, tm, LANES), lambda b, i: (0, i, 0)),
        ],
        out_specs=[row(MLA_HEADS * MLA_PAD), row(MLA_HEADS * MLA_PAD),
                   pl.BlockSpec((1, MLA_HEADS * MLA_V, tm), lambda b, i: (b, 0, i))] + [
            pl.BlockSpec((1, tm // d, d * 3 * DIL_W), lambda b, i: (b, i, 0)) for _, d in DIL_PATTERNS],
        out_shape=[
            jax.ShapeDtypeStruct((B, S, MLA_HEADS * MLA_PAD), BF16),
            jax.ShapeDtypeStruct((B, S, MLA_HEADS * MLA_PAD), BF16),
            jax.ShapeDtypeStruct((B, MLA_HEADS * MLA_V, S), BF16),
        ] + [jax.ShapeDtypeStruct((B, S // d, d * 3 * DIL_W), BF16) for _, d in DIL_PATTERNS],
        scratch_shapes=[pltpu.VMEM((DIL_W // LANES, tm, LANES), F32)],
        compiler_params=_cparams(("parallel", "parallel")),
    )(x, g, wp["w_mla"], wp["w_dil"], wp["q_norm"], wp["w_uq"], wp["kv_norm"], wp["w_uk"], wp["w_uv"], tabs)


def _mla_kernel(q_ref, k_ref, vt_ref, o_ref, m_scr, acc_scr, *, nk, cq):
    ki = pl.program_id(3)

    @pl.when(ki == 0)
    def _():
        m_scr[...] = jnp.full(m_scr.shape, -jnp.inf, F32)
        acc_scr[...] = jnp.zeros(acc_scr.shape, F32)

    tq = q_ref.shape[1]
    vt = vt_ref[0]
    own_rows = lax.broadcasted_iota(jnp.int32, vt.shape, 0) < MLA_V
    ones = jnp.ones(vt.shape, BF16)
    v1 = [jnp.where(own_rows, vt, ones), jnp.where(own_rows, ones, vt)]

    def scores(hh, c):
        q = q_ref[0, c * cq:(c + 1) * cq, hh * MLA_PAD:(hh + 1) * MLA_PAD]
        k = k_ref[0, :, hh * MLA_PAD:(hh + 1) * MLA_PAD]
        st = lax.dot_general(k, q, (((1,), (1,)), ((), ())), preferred_element_type=F32)
        m_prev = m_scr[hh, :, c * cq:(c + 1) * cq]
        m_new = jnp.maximum(m_prev, jnp.max(st, axis=0, keepdims=True))
        m_scr[hh, :, c * cq:(c + 1) * cq] = m_new
        return st, m_prev, m_new

    def accumulate(hh, c, st, m_prev, m_new):
        alpha = jnp.exp2(m_prev - m_new)
        p = jnp.exp2(st - m_new).astype(BF16)
        acc_scr[hh, :, c * cq:(c + 1) * cq] = alpha * acc_scr[hh, :, c * cq:(c + 1) * cq] + jnp.dot(
            v1[hh], p, preferred_element_type=F32)

    tasks = [(hh, c) for hh in range(2) for c in range(tq // cq)]
    pending = scores(*tasks[0])
    for t, task in enumerate(tasks):
        nxt = scores(*tasks[t + 1]) if t + 1 < len(tasks) else None
        accumulate(*task, *pending)
        pending = nxt

    @pl.when(ki == nk - 1)
    def _():
        a0, a1 = acc_scr[0], acc_scr[1]
        ot = jnp.concatenate([a0[:MLA_V] / a0[MLA_V:MLA_V + 1], a1[MLA_V:] / a1[0:1]], axis=0)
        o_ref[0] = ot.T.astype(BF16)


def _mla_attention(q, k, vt, tq, tk, cq=1024):
    B, S, _ = q.shape
    nq, nk = S // tq, S // tk
    hp = MLA_HEADS // 2
    return pl.pallas_call(
        functools.partial(_mla_kernel, nk=nk, cq=cq),
        grid=(B, hp, nq, nk),
        in_specs=[
            pl.BlockSpec((1, tq, 2 * MLA_PAD), lambda b, h, i, j: (b, i, h)),
            pl.BlockSpec((1, tk, 2 * MLA_PAD), lambda b, h, i, j: (b, j, h)),
            pl.BlockSpec((1, 2 * MLA_V, tk), lambda b, h, i, j: (b, h, j)),
        ],
        out_specs=pl.BlockSpec((1, tq, 2 * MLA_V), lambda b, h, i, j: (b, i, h)),
        out_shape=jax.ShapeDtypeStruct((B, S, MLA_HEADS * MLA_V), BF16),
        scratch_shapes=[
            pltpu.VMEM((2, 1, tq), F32),
            pltpu.VMEM((2, 2 * MLA_V, tq), F32),
        ],
        compiler_params=_cparams(("parallel", "parallel", "parallel", "arbitrary")),
    )(q, k, vt)


def _dil_kernel(prev_ref, cur_ref, next_ref, o_ref, lse_ref, *, nt, tt, nres):
    i = pl.program_id(2)
    nsub = tt // DIL_QB
    nkx = DIL_QB + 2 * DIL_STEPS
    qi = lax.broadcasted_iota(jnp.int32, (DIL_QB, nkx), 0)
    kx = lax.broadcasted_iota(jnp.int32, (DIL_QB, nkx), 1)
    rel = kx - DIL_STEPS - qi
    band = (rel >= -DIL_STEPS) & (rel <= DIL_STEPS)
    lane_q = lax.broadcasted_iota(jnp.int32, (DIL_QB, LANES), 1)
    first = lane_q < HEAD_DIM
    zero = jnp.zeros((DIL_QB, LANES), BF16)
    for res in range(nres):
        qc, kc, vc = (res * 3 * DIL_W + c * DIL_W for c in range(3))
        kext = jnp.concatenate([r[0, :, kc:kc + DIL_W] for r in (prev_ref, cur_ref, next_ref)], axis=0)
        vext = jnp.concatenate([r[0, :, vc:vc + DIL_W] for r in (prev_ref, cur_ref, next_ref)], axis=0)
        for j in range(nsub):
            ok = band
            if j == 0:
                ok = ok & ((kx >= DIL_STEPS) | (i > 0))
            if j == nsub - 1:
                ok = ok & ((kx < DIL_QB + DIL_STEPS) | (i < nt - 1))
            rows = slice(j * DIL_QB, (j + 1) * DIL_QB)
            krows = slice(j * DIL_QB, j * DIL_QB + nkx)
            for hp in range(DIL_HEADS // 2):
                sl = slice(hp * LANES, (hp + 1) * LANES)
                qp = cur_ref[0, rows, qc + hp * LANES:qc + (hp + 1) * LANES]
                kp = kext[krows, sl]
                vp = vext[krows, sl]
                outs, lses = [], []
                for hh in range(2):
                    qh = jnp.where(first if hh == 0 else ~first, qp, zero)
                    s = lax.dot_general(qh, kp, (((1,), (1,)), ((), ())), preferred_element_type=F32)
                    s = jnp.where(ok, s, NEG_BIG)
                    m = jnp.max(s, axis=1, keepdims=True)
                    e = jnp.exp2(s - m)
                    den = jnp.sum(e, axis=1, keepdims=True)
                    outs.append(jnp.dot(e.astype(BF16), vp, preferred_element_type=F32) / den)
                    lses.append(jnp.broadcast_to(m + jnp.log2(den), (DIL_QB, LANES)))
                oc = res * DIL_W + hp * LANES
                o_ref[0, rows, oc:oc + LANES] = jnp.where(first, outs[0], outs[1])
                lse_ref[0, rows, oc:oc + LANES] = jnp.where(first, lses[0], lses[1])


def _dil_attention(x, dil, tt):
    B, L, _ = x.shape
    nt = L // tt
    nres = max(1, min(dil, DIL_ROWS_PER_STEP // tt))
    hb = tt // DIL_STEPS
    nhb = L // DIL_STEPS
    w_in, w_out = nres * 3 * DIL_W, nres * DIL_W
    cur = pl.BlockSpec((1, tt, w_in), lambda b, r, i: (b, i, r))
    prev = pl.BlockSpec((1, DIL_STEPS, w_in), lambda b, r, i: (b, jnp.maximum(i * hb - 1, 0), r))
    nxt = pl.BlockSpec((1, DIL_STEPS, w_in), lambda b, r, i: (b, jnp.minimum((i + 1) * hb, nhb - 1), r))
    out_spec = pl.BlockSpec((1, tt, w_out), lambda b, r, i: (b, i, r))
    return pl.pallas_call(
        functools.partial(_dil_kernel, nt=nt, tt=tt, nres=nres),
        grid=(B, dil // nres, nt),
        in_specs=[prev, cur, nxt],
        out_specs=[out_spec, out_spec],
        out_shape=[jax.ShapeDtypeStruct((B, L, dil * DIL_W), F32)] * 2,
        compiler_params=_cparams(("parallel", "parallel", "parallel")),
    )(x, x, x)


def _ffn(x1, gf, wgu_ref, wdown_ref):
    h2 = _rms(x1, gf).astype(BF16)
    acc = x1
    for c in range(FF_CHUNKS):
        gate = jnp.dot(h2, wgu_ref[:, c * FF_CHUNK:(c + 1) * FF_CHUNK], preferred_element_type=F32)
        up = jnp.dot(h2, wgu_ref[:, D_FF + c * FF_CHUNK:D_FF + (c + 1) * FF_CHUNK],
                     preferred_element_type=F32)
        act = (gate * jax.nn.sigmoid(gate) * up).astype(BF16)
        acc = acc + jnp.dot(act, wdown_ref[c * FF_CHUNK:(c + 1) * FF_CHUNK, :], preferred_element_type=F32)
    return acc


def _out_ffn_attn_kernel(x_ref, om_ref, o0_ref, o1_ref, o2_ref, l0_ref, l1_ref, l2_ref,
                         wo_ref, gf_ref, wgu_ref, wdown_ref, y_ref, perm_scr):
    tm = x_ref.shape[0]

    def natural(ref, slot, dil):
        for r in range(dil):
            for b in range(DIL_W // LANES):
                perm_scr[slot, b, pl.ds(r, tm // dil, stride=dil), :] = ref[
                    :, r * DIL_W + b * LANES:r * DIL_W + (b + 1) * LANES]
        return jnp.concatenate([perm_scr[slot, b] for b in range(DIL_W // LANES)], axis=1)

    d1, d2 = DIL_PATTERNS[1][1], DIL_PATTERNS[2][1]
    l0, l1, l2 = l0_ref[...], natural(l1_ref, 0, d1), natural(l2_ref, 1, d2)
    mx = jnp.maximum(jnp.maximum(l0, l1), l2)
    e0, e1, e2 = jnp.exp2(l0 - mx), jnp.exp2(l1 - mx), jnp.exp2(l2 - mx)
    od = (e0 * o0_ref[...] + e1 * natural(o1_ref, 2, d1) + e2 * natural(o2_ref, 3, d2)) / (e0 + e1 + e2)
    o = jnp.concatenate([om_ref[...], od.astype(BF16)], axis=1)
    x1 = x_ref[...] + jnp.dot(o, wo_ref[...], preferred_element_type=F32)
    y_ref[...] = _ffn(x1, gf_ref[...], wgu_ref, wdown_ref)


def _out_ffn_rec_kernel(x_ref, a_ref, wo_ref, gf_ref, wgu_ref, wdown_ref, gfin_ref, y_ref):
    x1 = x_ref[...] + jnp.dot(a_ref[...], wo_ref[...], preferred_element_type=F32)
    y_ref[...] = _rms(_ffn(x1, gf_ref[...], wgu_ref, wdown_ref), gfin_ref[...])


def _out_ffn_attn(x, om, os_, ls_, wp, layer, tm):
    T, D = x.shape
    row = lambda w: pl.BlockSpec((tm, w), lambda i: (i, 0))
    grp = [pl.BlockSpec((tm // d, d * DIL_W), lambda i: (i, 0)) for _, d in DIL_PATTERNS]
    return pl.pallas_call(
        _out_ffn_attn_kernel,
        grid=(T // tm,),
        in_specs=[row(D), row(DIL_W)] + grp + grp + [
            _const_spec((MLA_HEADS * MLA_V + DIL_W, D)), _const_spec((1, D)),
            _const_spec((D, 2 * D_FF)), _const_spec((D_FF, D))],
        out_specs=row(D),
        out_shape=jax.ShapeDtypeStruct((T, D), F32),
        scratch_shapes=[pltpu.VMEM((4, DIL_W // LANES, tm, LANES), F32)],
        compiler_params=_cparams(("parallel",)),
    )(x, om, *os_, *ls_, wp["w_out_a"], wp["norm_ffn"][layer],
      wp["w_gu"][layer], wp["w_down"][layer])


def _out_ffn_rec(x, a, wp, layer, tm):
    T, D = x.shape
    row = lambda w: pl.BlockSpec((tm, w), lambda i: (i, 0))
    return pl.pallas_call(
        _out_ffn_rec_kernel,
        grid=(T // tm,),
        in_specs=[row(D), row(D_RNN), _const_spec((D_RNN, D)), _const_spec((1, D)),
                  _const_spec((D, 2 * D_FF)), _const_spec((D_FF, D)), _const_spec((1, D))],
        out_specs=row(D),
        out_shape=jax.ShapeDtypeStruct((T, D), F32),
        compiler_params=_cparams(("parallel",)),
    )(x, a, wp["w_out_r"], wp["norm_ffn"][layer], wp["w_gu"][layer], wp["w_down"][layer],
      wp["norm_final"])


def _l1_prep_kernel(x_ref, g_ref, w_ref, y_ref, xr_ref):
    h = _rms(x_ref[...], g_ref[...]).astype(BF16)
    zy = jnp.dot(h, w_ref[:, :D_RNN], preferred_element_type=F32)
    y_ref[...] = (0.5 * zy * (1.0 + jnp.tanh(np.sqrt(2.0 / np.pi).astype(np.float32)
                                             * (zy + 0.044715 * (zy * zy * zy))))).astype(BF16)
    xr_ref[...] = jnp.dot(h, w_ref[:, D_RNN:], preferred_element_type=F32)


def _l1_prep(x, g, w, tm):
    T, D = x.shape
    row = lambda w_: pl.BlockSpec((tm, w_), lambda i: (i, 0))
    return pl.pallas_call(
        _l1_prep_kernel,
        grid=(T // tm,),
        in_specs=[row(D), _const_spec((1, D)), _const_spec((D, 2 * D_RNN))],
        out_specs=[row(D_RNN), row(D_RNN)],
        out_shape=[jax.ShapeDtypeStruct((T, D_RNN), BF16), jax.ShapeDtypeStruct((T, D_RNN), F32)],
        compiler_params=_cparams(("parallel",)),
    )(x, g, w)


def _rglru_kernel(*refs, ns, ts, cb, reverse):
    if reverse:
        (xp_ref, xc_ref, xn_ref, cw_ref, cbias_ref, wg_ref, br_ref, bi_ref, lam_ref, hf_ref, y_ref,
         o_ref, a_scr, u_scr, h_scr, carry_scr) = refs
    else:
        (xp_ref, xc_ref, xn_ref, cw_ref, cbias_ref, wg_ref, br_ref, bi_ref, lam_ref,
         o_ref, a_scr, u_scr, carry_scr) = refs
        h_scr = o_ref.at[0]
    step = pl.program_id(2)
    tile = (ns - 1 - step) if reverse else step

    @pl.when(step == 0)
    def _():
        carry_scr[...] = jnp.zeros(carry_scr.shape, F32)

    x0 = xc_ref[0]
    rows = lax.broadcasted_iota(jnp.int32, (ts, cb), 0)
    pv = jnp.where(tile > 0, xp_ref[0], 0.0)
    nv = jnp.where(tile < ns - 1, xn_ref[0], 0.0)
    xm1 = jnp.where(rows == 0, pv[SUBLANES - 1:SUBLANES], pltpu.roll(x0, 1, 0))
    xm2 = pltpu.roll(x0, 2, 0)
    xm2 = jnp.where(rows == 0, pv[SUBLANES - 2:SUBLANES - 1], xm2)
    xm2 = jnp.where(rows == 1, pv[SUBLANES - 1:SUBLANES], xm2)
    xp1 = jnp.where(rows == ts - 1, nv[0:1], pltpu.roll(x0, ts - 1, 0))
    xc = (xm2 * cw_ref[0:1] + xm1 * cw_ref[1:2] + x0 * cw_ref[2:3] + xp1 * cw_ref[3:4]) + cbias_ref[...]

    xcb = xc.astype(BF16)
    for j in range(cb // LRU_BW):
        sl = slice(j * LRU_BW, (j + 1) * LRU_BW)
        gts = jnp.dot(xcb[:, sl], wg_ref[j], preferred_element_type=F32)
        r = jax.nn.sigmoid(gts[:, :LRU_BW] + br_ref[:, sl])
        ig = jax.nn.sigmoid(gts[:, LRU_BW:] + bi_ref[:, sl])
        nlam = -lam_ref[:, sl]
        softplus = jnp.maximum(nlam, 0.0) + jnp.log1p(jnp.exp(-jnp.abs(nlam)))
        a = jnp.exp((-LRU_C) * r * softplus)
        a_scr[:, sl] = a
        u_scr[:, sl] = jnp.sqrt(jnp.maximum(1.0 - a * a, 0.0)) * (ig * xc[:, sl])

    row8 = lax.broadcasted_iota(jnp.int32, (SUBLANES, cb), 0)
    nchunk = ts // SUBLANES

    def chunk(c, hprev):
        idx = (nchunk - 1 - c) if reverse else c
        r0 = pl.multiple_of(idx * SUBLANES, SUBLANES)
        a = a_scr[pl.ds(r0, SUBLANES), :]
        b = u_scr[pl.ds(r0, SUBLANES), :]
        for d in (1, 2, 4):
            if reverse:
                keep = row8 < SUBLANES - d
                sh = SUBLANES - d
            else:
                keep = row8 >= d
                sh = d
            a_s = jnp.where(keep, pltpu.roll(a, sh, 0), 1.0)
            b_s = jnp.where(keep, pltpu.roll(b, sh, 0), 0.0)
            b = a * b_s + b
            a = a * a_s
        h = a * hprev + b
        h_scr[pl.ds(r0, SUBLANES), :] = h
        edge = h[0:1] if reverse else h[SUBLANES - 1:SUBLANES]
        return jnp.broadcast_to(edge, (SUBLANES, cb))

    carry_scr[...] = lax.fori_loop(0, nchunk, chunk, carry_scr[...], unroll=4)

    if reverse:
        o_ref[0] = ((hf_ref[0] + h_scr[...]) * y_ref[0].astype(F32)).astype(BF16)


def _rglru(xr, wp, ts, cb, reverse, hf=None, y=None):
    B, S, C = xr.shape
    ns = S // ts
    nhb = S // SUBLANES
    hb = ts // SUBLANES
    d = 1 if reverse else 0
    tidx = (lambda i: ns - 1 - i) if reverse else (lambda i: i)
    cur = pl.BlockSpec((1, ts, cb), lambda b, c, i: (b, tidx(i), c))
    prev = pl.BlockSpec((1, SUBLANES, cb), lambda b, c, i: (b, jnp.maximum(tidx(i) * hb - 1, 0), c))
    nxt = pl.BlockSpec((1, SUBLANES, cb), lambda b, c, i: (b, jnp.minimum((tidx(i) + 1) * hb, nhb - 1), c))
    vec = pl.BlockSpec((1, cb), lambda b, c, i: (0, c))
    in_specs = [prev, cur, nxt,
                pl.BlockSpec((CONV_W, cb), lambda b, c, i: (0, c)), vec,
                pl.BlockSpec((cb // LRU_BW, LRU_BW, 2 * LRU_BW), lambda b, c, i: (c, 0, 0)),
                vec, vec, vec]
    args = [xr, xr, xr, wp["conv_w"], wp["conv_b"], wp["w_gate"][d], wp["b_r"][d], wp["b_i"][d],
            wp["lam"][d]]
    scratch = [pltpu.VMEM((ts, cb), F32), pltpu.VMEM((ts, cb), F32)]
    if reverse:
        in_specs += [cur, cur]
        args += [hf, y]
        scratch.append(pltpu.VMEM((ts, cb), F32))
        out_dtype = BF16
    else:
        out_dtype = F32
    scratch.append(pltpu.VMEM((SUBLANES, cb), F32))
    return pl.pallas_call(
        functools.partial(_rglru_kernel, ns=ns, ts=ts, cb=cb, reverse=reverse),
        grid=(B, C // cb, ns),
        in_specs=in_specs,
        out_specs=cur,
        out_shape=jax.ShapeDtypeStruct((B, S, C), out_dtype),
        scratch_shapes=scratch,
        compiler_params=_cparams(("parallel", "parallel", "arbitrary")),
    )(*args)


def _lane_table(vals, jidx, mask, fill):
    return jnp.where(mask[None, :], vals[:, jidx], fill)


def _rope_tables(S, half, period, start, scale):
    inv = jnp.power(ROPE_THETA, -jnp.arange(half, dtype=F32) / half)
    ang = jnp.arange(S, dtype=F32)[:, None] * inv[None, :]
    cos, sin = jnp.cos(ang), jnp.sin(ang)
    e = np.arange(LANES) % period - start
    first = (e >= 0) & (e < half)
    second = (e >= half) & (e < 2 * half)
    jidx = np.where(first | second, e % half, 0)
    c = _lane_table(cos, jidx, first | second, 1.0)
    s1 = _lane_table(-sin, jidx, first, 0.0)
    s2 = _lane_table(sin, jidx, second, 0.0)
    return jnp.stack([c, s1, s2]) * scale


def _all_tables(S):
    mla_scale = float((MLA_NOPE + MLA_ROPE) ** -0.5 * np.log2(np.e))
    dil_scale = float(HEAD_DIM ** -0.5 * np.log2(np.e))
    return jnp.concatenate([
        _rope_tables(S, MLA_ROPE // 2, MLA_PAD, MLA_NOPE, mla_scale),
        _rope_tables(S, MLA_ROPE // 2, MLA_PAD, MLA_NOPE, 1.0),
        _rope_tables(S, DIL_ROT // 2, HEAD_DIM, 0, dil_scale),
        _rope_tables(S, DIL_ROT // 2, HEAD_DIM, 0, 1.0),
    ])


def _prep_weights(norm_mix, w_in_a, q_norm, w_uq, kv_norm, w_ukv, w_out_a, w_in_r, conv_w, conv_b,
                  lru_w_gate, lru_b_gate, lru_lambda, w_out_r, norm_ffn, w_gu, w_down, norm_final):
    w_in = w_in_a[0]
    qk_dim = MLA_NOPE + MLA_ROPE
    kr_cols = jnp.pad(w_in[:, MLA_Q_RANK + MLA_KV_RANK:MLA_IN],
                      ((0, 0), (MLA_NOPE, MLA_PAD - qk_dim)))
    w_mla = jnp.concatenate([w_in[:, :MLA_Q_RANK + MLA_KV_RANK], kr_cols], axis=1)
    uq = w_uq[0].reshape(MLA_Q_RANK, MLA_HEADS, qk_dim)
    uq = jnp.pad(uq, ((0, 0), (0, 0), (0, MLA_PAD - qk_dim))).reshape(MLA_Q_RANK, MLA_HEADS * MLA_PAD)
    ukv = w_ukv[0].reshape(MLA_KV_RANK, MLA_HEADS, MLA_NOPE + MLA_V)
    uk = jnp.pad(ukv[:, :, :MLA_NOPE], ((0, 0), (0, 0), (0, MLA_PAD - MLA_NOPE)))
    uk = uk.reshape(MLA_KV_RANK, MLA_HEADS * MLA_PAD)
    uv = ukv[:, :, MLA_NOPE:].reshape(MLA_KV_RANK, MLA_HEADS * MLA_V).T
    wg = lru_w_gate[0]
    w_gate = jnp.concatenate([wg[:, 0], wg[:, 1]], axis=-1)
    row = lambda v: v.reshape(1, -1)
    return {
        "norm_mix": [row(norm_mix[l]) for l in range(2)],
        "w_mla": w_mla.astype(BF16),
        "w_dil": w_in[:, MLA_IN:].astype(BF16),
        "q_norm": row(q_norm[0]),
        "w_uq": uq.astype(BF16),
        "kv_norm": row(kv_norm[0]),
        "w_uk": uk.astype(BF16),
        "w_uv": uv.astype(BF16),
        "w_out_a": w_out_a[0].astype(BF16),
        "w_in_r": w_in_r[0].astype(BF16),
        "conv_w": conv_w[0],
        "conv_b": row(conv_b[0]),
        "w_gate": w_gate.astype(BF16),
        "b_r": [row(lru_b_gate[0, d, 0]) for d in range(2)],
        "b_i": [row(lru_b_gate[0, d, 1]) for d in range(2)],
        "lam": [row(lru_lambda[0, d]) for d in range(2)],
        "w_out_r": w_out_r[0].astype(BF16),
        "norm_ffn": [row(norm_ffn[l]) for l in range(2)],
        "w_gu": [w_gu[l].astype(BF16) for l in range(2)],
        "w_down": [w_down[l].astype(BF16) for l in range(2)],
        "norm_final": row(norm_final),
    }


def _trunk(x, wp):
    B, S, D = x.shape
    T = B * S
    tabs = _all_tables(S)
    q, k, vt, *dgs = _l0_prep(x, wp["norm_mix"][0], wp, tabs, tm=512)
    o_mla = _mla_attention(q, k, vt, tq=min(S, 2048), tk=min(S, 2048))
    os_, ls_ = [], []
    for dg, (_, dil) in zip(dgs, DIL_PATTERNS):
        o_g, l_g = _dil_attention(dg, dil, tt=min(S // dil, 512))
        os_.append(o_g.reshape(T // dil, dil * DIL_W))
        ls_.append(l_g.reshape(T // dil, dil * DIL_W))
    x1 = _out_ffn_attn(x.reshape(T, D), o_mla.reshape(T, MLA_HEADS * MLA_V), os_, ls_, wp, 0, tm=512)
    y, xr = _l1_prep(x1, wp["norm_mix"][1], wp["w_in_r"], tm=512)
    xr = xr.reshape(B, S, D_RNN)
    hf = _rglru(xr, wp, ts=512, cb=D_RNN, reverse=False)
    a = _rglru(xr, wp, ts=512, cb=D_RNN, reverse=True, hf=hf, y=y.reshape(B, S, D_RNN))
    out = _out_ffn_rec(x1, a.reshape(T, D_RNN), wp, 1, tm=512)
    return out.reshape(B, S, D)


def kernel(x_prompt, x_sample, norm_mix, w_in_a, q_norm, w_uq, kv_norm, w_ukv, w_out_a, w_in_r, conv_w,
           conv_b, lru_w_gate, lru_b_gate, lru_lambda, w_out_r, norm_ffn, w_gu, w_down, norm_final):
    wp = _prep_weights(norm_mix, w_in_a, q_norm, w_uq, kv_norm, w_ukv, w_out_a, w_in_r, conv_w, conv_b,
                       lru_w_gate, lru_b_gate, lru_lambda, w_out_r, norm_ffn, w_gu, w_down, norm_final)
    return (_trunk(x_prompt, wp), _trunk(x_sample, wp))
```

```python
import functools

import numpy as np
import jax
import jax.numpy as jnp
from jax import lax
from jax.experimental import pallas as pl
from jax.experimental.pallas import tpu as pltpu

F32 = jnp.float32
BF16 = jnp.bfloat16

D_MODEL = 1024
HEAD_DIM = 64
ROPE_THETA = 500000.0
NORM_EPS = 1e-6
MLA_HEADS = 8
MLA_NOPE = 64
MLA_ROPE = 32
MLA_V = 64
MLA_Q_RANK = 256
MLA_KV_RANK = 128
DIL_PATTERNS = ((128, 1), (512, 4), (2048, 16))
DIL_GROUPS = len(DIL_PATTERNS)
DIL_HEADS = 8
DIL_ROT = HEAD_DIM // 4
DIL_STEPS = 64
DIL_QB = 128
DIL_ROWS_PER_STEP = 512
MLA_IN = MLA_Q_RANK + MLA_KV_RANK + MLA_ROPE
DIL_W = DIL_HEADS * HEAD_DIM
DIL_QKV = 3 * DIL_GROUPS * DIL_W
D_RNN = 1536
LRU_BLOCKS = 12
LRU_BW = D_RNN // LRU_BLOCKS
CONV_W = 4
CONV_LEFT = 2
LRU_C = 8.0
D_FF = ((8 * D_MODEL // 3 + 255) // 256) * 256
NEG_BIG = -1e30

LANES = 128
SUBLANES = 8
MLA_PAD = 128
MLA_DEN_ROWS = 16
VMEM_LIMIT = 56 * 1024 * 1024

FF_CHUNK = 256
FF_CHUNKS = D_FF // FF_CHUNK


def _cparams(sem):
    return pltpu.CompilerParams(dimension_semantics=sem, vmem_limit_bytes=VMEM_LIMIT)


def _const_spec(shape):
    nd = len(shape)
    return pl.BlockSpec(shape, lambda *_: (0,) * nd, pipeline_mode=pl.Buffered(1))


def _rms(x, g):
    return x * lax.rsqrt(jnp.mean(x * x, axis=-1, keepdims=True) + NORM_EPS) * g


def _rope_block(xb, c, s1, s2, half):
    return xb * c + pltpu.roll(xb, LANES - half, 1) * s1 + pltpu.roll(xb, half, 1) * s2


def _l0_prep_kernel(x_ref, g_ref, wmla_ref, wdil_ref, qn_ref, wuq_ref, kvn_ref, wuk_ref, wuv_ref,
                    tab_ref, q_out, k_out, v_out, d0_out, d1_out, d2_out, perm_scr):
    tm = x_ref.shape[1]
    d_outs = (d0_out, d1_out, d2_out)
    x = x_ref[0]
    h = _rms(x, g_ref[...]).astype(BF16)
    z = jnp.dot(h, wmla_ref[...], preferred_element_type=F32)
    qn = _rms(z[:, :MLA_Q_RANK], qn_ref[...]).astype(BF16)
    kvn = _rms(z[:, MLA_Q_RANK:MLA_Q_RANK + MLA_KV_RANK], kvn_ref[...]).astype(BF16)
    kr = z[:, MLA_Q_RANK + MLA_KV_RANK:]
    qf = jnp.dot(qn, wuq_ref[...], preferred_element_type=F32)
    kf = jnp.dot(kvn, wuk_ref[...], preferred_element_type=F32)
    vt = lax.dot_general(wuv_ref[...], kvn, (((1,), (1,)), ((), ())),
                         preferred_element_type=F32)
    half = MLA_ROPE // 2
    krr = _rope_block(kr, tab_ref[3], tab_ref[4], tab_ref[5], half)
    for hh in range(MLA_HEADS):
        sl = slice(hh * MLA_PAD, (hh + 1) * MLA_PAD)
        q_out[0, :, sl] = _rope_block(qf[:, sl], tab_ref[0], tab_ref[1], tab_ref[2], half).astype(BF16)
        k_out[0, :, sl] = (kf[:, sl] + krr).astype(BF16)
    v_out[0] = vt.astype(BF16)
    dhalf = DIL_ROT // 2
    for j in range(3 * DIL_GROUPS):
        c, g = divmod(j, DIL_GROUPS)
        dil = DIL_PATTERNS[g][1]
        zc = jnp.dot(h, wdil_ref[:, j * DIL_W:(j + 1) * DIL_W], preferred_element_type=F32)
        for b in range(DIL_W // LANES):
            zb = zc[:, b * LANES:(b + 1) * LANES]
            if c < 2:
                t0 = 6 + 3 * c
                zb = _rope_block(zb, tab_ref[t0], tab_ref[t0 + 1], tab_ref[t0 + 2], dhalf)
            if dil == 1:
                d_outs[g][0, :, c * DIL_W + b * LANES:c * DIL_W + (b + 1) * LANES] = zb.astype(BF16)
            else:
                perm_scr[b] = zb
        if dil > 1:
            for r in range(dil):
                for b in range(DIL_W // LANES):
                    col = (r * 3 + c) * DIL_W + b * LANES
                    d_outs[g][0, :, col:col + LANES] = perm_scr[
                        b, pl.ds(r, tm // dil, stride=dil), :].astype(BF16)


def _l0_prep(x, g, wp, tabs, tm):
    B, S, D = x.shape
    grid = (B, S // tm)
    row = lambda w: pl.BlockSpec((1, tm, w), lambda b, i: (b, i, 0))
    return pl.pallas_call(
        _l0_prep_kernel,
        grid=grid,
        in_specs=[
            row(D),
            _const_spec((1, D)),
            _const_spec(wp["w_mla"].shape),
            _const_spec(wp["w_dil"].shape),
            _const_spec((1, MLA_Q_RANK)),
            _const_spec(wp["w_uq"].shape),
            _const_spec((1, MLA_KV_RANK)),
            _const_spec(wp["w_uk"].shape),
            _const_spec(wp["w_uv"].shape),
            pl.BlockSpec((12, tm, LANES), lambda b, i: (0, i, 0)),
        ],
        out_specs=[row(MLA_HEADS * MLA_PAD), row(MLA_HEADS * MLA_PAD),
                   pl.BlockSpec((1, MLA_HEADS * MLA_V, tm), lambda b, i: (b, 0, i))] + [
            pl.BlockSpec((1, tm // d, d * 3 * DIL_W), lambda b, i: (b, i, 0)) for _, d in DIL_PATTERNS],
        out_shape=[
            jax.ShapeDtypeStruct((B, S, MLA_HEADS * MLA_PAD), BF16),
            jax.ShapeDtypeStruct((B, S, MLA_HEADS * MLA_PAD), BF16),
            jax.ShapeDtypeStruct((B, MLA_HEADS * MLA_V, S), BF16),
        ] + [jax.ShapeDtypeStruct((B, S // d, d * 3 * DIL_W), BF16) for _, d in DIL_PATTERNS],
        scratch_shapes=[pltpu.VMEM((DIL_W // LANES, tm, LANES), F32)],
        compiler_params=_cparams(("parallel", "parallel")),
    )(x, g, wp["w_mla"], wp["w_dil"], wp["q_norm"], wp["w_uq"], wp["kv_norm"], wp["w_uk"], wp["w_uv"], tabs)


def _mla_kernel(q_ref, k_ref, vt_ref, o_ref, m_scr, acc_scr, *, nk, cq):
    ki = pl.program_id(3)

    @pl.when(ki == 0)
    def _():
        m_scr[...] = jnp.full(m_scr.shape, -jnp.inf, F32)
        acc_scr[...] = jnp.zeros(acc_scr.shape, F32)

    tq = q_ref.shape[1]
    ones = jnp.ones((MLA_DEN_ROWS, vt_ref.shape[2]), BF16)
    v1 = [jnp.concatenate([vt_ref[0, hh * MLA_V:(hh + 1) * MLA_V, :], ones], axis=0)
          for hh in range(2)]

    def scores(hh, c):
        q = q_ref[0, c * cq:(c + 1) * cq, hh * MLA_PAD:(hh + 1) * MLA_PAD]
        k = k_ref[0, :, hh * MLA_PAD:(hh + 1) * MLA_PAD]
        st = lax.dot_general(k, q, (((1,), (1,)), ((), ())), preferred_element_type=F32)
        m_prev = m_scr[hh, :, c * cq:(c + 1) * cq]
        m_new = jnp.maximum(m_prev, jnp.max(st, axis=0, keepdims=True))
        m_scr[hh, :, c * cq:(c + 1) * cq] = m_new
        return st, m_prev, m_new

    def accumulate(hh, c, st, m_prev, m_new):
        alpha = jnp.exp2(m_prev - m_new)
        p = jnp.exp2(st - m_new).astype(BF16)
        acc_scr[hh, :, c * cq:(c + 1) * cq] = alpha * acc_scr[hh, :, c * cq:(c + 1) * cq] + jnp.dot(
            v1[hh], p, preferred_element_type=F32)

    tasks = [(hh, c) for hh in range(2) for c in range(tq // cq)]
    pending = scores(*tasks[0])
    for t, task in enumerate(tasks):
        nxt = scores(*tasks[t + 1]) if t + 1 < len(tasks) else None
        accumulate(*task, *pending)
        pending = nxt

    @pl.when(ki == nk - 1)
    def _():
        a0, a1 = acc_scr[0], acc_scr[1]
        ot = jnp.concatenate([a0[:MLA_V] / a0[MLA_V:MLA_V + 1], a1[:MLA_V] / a1[MLA_V:MLA_V + 1]], axis=0)
        o_ref[0] = ot.T.astype(BF16)


def _mla_attention(q, k, vt, tq, tk, cq=1024):
    B, S, _ = q.shape
    nq, nk = S // tq, S // tk
    hp = MLA_HEADS // 2
    return pl.pallas_call(
        functools.partial(_mla_kernel, nk=nk, cq=cq),
        grid=(B, hp, nq, nk),
        in_specs=[
            pl.BlockSpec((1, tq, 2 * MLA_PAD), lambda b, h, i, j: (b, i, h)),
            pl.BlockSpec((1, tk, 2 * MLA_PAD), lambda b, h, i, j: (b, j, h)),
            pl.BlockSpec((1, 2 * MLA_V, tk), lambda b, h, i, j: (b, h, j)),
        ],
        out_specs=pl.BlockSpec((1, tq, 2 * MLA_V), lambda b, h, i, j: (b, i, h)),
        out_shape=jax.ShapeDtypeStruct((B, S, MLA_HEADS * MLA_V), BF16),
        scratch_shapes=[
            pltpu.VMEM((2, 1, tq), F32),
            pltpu.VMEM((2, MLA_V + MLA_DEN_ROWS, tq), F32),
        ],
        compiler_params=_cparams(("parallel", "parallel", "parallel", "arbitrary")),
    )(q, k, vt)


def _dil_kernel(prev_ref, cur_ref, next_ref, o_ref, lse_ref, *, nt, tt, nres):
    i = pl.program_id(2)
    nsub = tt // DIL_QB
    nkx = DIL_QB + 2 * DIL_STEPS
    qi = lax.broadcasted_iota(jnp.int32, (DIL_QB, nkx), 0)
    kx = lax.broadcasted_iota(jnp.int32, (DIL_QB, nkx), 1)
    rel = kx - DIL_STEPS - qi
    band = (rel >= -DIL_STEPS) & (rel <= DIL_STEPS)
    lane_q = lax.broadcasted_iota(jnp.int32, (DIL_QB, LANES), 1)
    first = lane_q < HEAD_DIM
    zero = jnp.zeros((DIL_QB, LANES), BF16)
    for res in range(nres):
        qc, kc, vc = (res * 3 * DIL_W + c * DIL_W for c in range(3))
        kext = jnp.concatenate([r[0, :, kc:kc + DIL_W] for r in (prev_ref, cur_ref, next_ref)], axis=0)
        vext = jnp.concatenate([r[0, :, vc:vc + DIL_W] for r in (prev_ref, cur_ref, next_ref)], axis=0)
        for j in range(nsub):
            ok = band
            if j == 0:
                ok = ok & ((kx >= DIL_STEPS) | (i > 0))
            if j == nsub - 1:
                ok = ok & ((kx < DIL_QB + DIL_STEPS) | (i < nt - 1))
            rows = slice(j * DIL_QB, (j + 1) * DIL_QB)
            krows = slice(j * DIL_QB, j * DIL_QB + nkx)
            for hp in range(DIL_HEADS // 2):
                sl = slice(hp * LANES, (hp + 1) * LANES)
                qp = cur_ref[0, rows, qc + hp * LANES:qc + (hp + 1) * LANES]
                kp = kext[krows, sl]
                vp = vext[krows, sl]
                outs, lses = [], []
                for hh in range(2):
                    qh = jnp.where(first if hh == 0 else ~first, qp, zero)
                    s = lax.dot_general(qh, kp, (((1,), (1,)), ((), ())), preferred_element_type=F32)
                    s = jnp.where(ok, s, NEG_BIG)
                    m = jnp.max(s, axis=1, keepdims=True)
                    e = jnp.exp2(s - m)
                    den = jnp.sum(e, axis=1, keepdims=True)
                    outs.append(jnp.dot(e.astype(BF16), vp, preferred_element_type=F32) / den)
                    lses.append(jnp.broadcast_to(m + jnp.log2(den), (DIL_QB, LANES)))
                oc = res * DIL_W + hp * LANES
                o_ref[0, rows, oc:oc + LANES] = jnp.where(first, outs[0], outs[1])
                lse_ref[0, rows, oc:oc + LANES] = jnp.where(first, lses[0], lses[1])


def _dil_attention(x, dil, tt):
    B, L, _ = x.shape
    nt = L // tt
    nres = max(1, min(dil, DIL_ROWS_PER_STEP // tt))
    hb = tt // DIL_STEPS
    nhb = L // DIL_STEPS
    w_in, w_out = nres * 3 * DIL_W, nres * DIL_W
    cur = pl.BlockSpec((1, tt, w_in), lambda b, r, i: (b, i, r))
    prev = pl.BlockSpec((1, DIL_STEPS, w_in), lambda b, r, i: (b, jnp.maximum(i * hb - 1, 0), r))
    nxt = pl.BlockSpec((1, DIL_STEPS, w_in), lambda b, r, i: (b, jnp.minimum((i + 1) * hb, nhb - 1), r))
    out_spec = pl.BlockSpec((1, tt, w_out), lambda b, r, i: (b, i, r))
    return pl.pallas_call(
        functools.partial(_dil_kernel, nt=nt, tt=tt, nres=nres),
        grid=(B, dil // nres, nt),
        in_specs=[prev, cur, nxt],
        out_specs=[out_spec, out_spec],
        out_shape=[jax.ShapeDtypeStruct((B, L, dil * DIL_W), F32)] * 2,
        compiler_params=_cparams(("parallel", "parallel", "parallel")),
    )(x, x, x)


def _ffn(x1, gf, wgu_ref, wdown_ref):
    h2 = _rms(x1, gf).astype(BF16)
    acc = x1
    for c in range(FF_CHUNKS):
        gate = jnp.dot(h2, wgu_ref[:, c * FF_CHUNK:(c + 1) * FF_CHUNK], preferred_element_type=F32)
        up = jnp.dot(h2, wgu_ref[:, D_FF + c * FF_CHUNK:D_FF + (c + 1) * FF_CHUNK],
                     preferred_element_type=F32)
        act = (gate * jax.nn.sigmoid(gate) * up).astype(BF16)
        acc = acc + jnp.dot(act, wdown_ref[c * FF_CHUNK:(c + 1) * FF_CHUNK, :], preferred_element_type=F32)
    return acc


def _out_ffn_attn_kernel(x_ref, om_ref, o0_ref, o1_ref, o2_ref, l0_ref, l1_ref, l2_ref,
                         wo_ref, gf_ref, wgu_ref, wdown_ref, y_ref, perm_scr):
    tm = x_ref.shape[0]

    def natural(ref, slot, dil):
        for r in range(dil):
            for b in range(DIL_W // LANES):
                perm_scr[slot, b, pl.ds(r, tm // dil, stride=dil), :] = ref[
                    :, r * DIL_W + b * LANES:r * DIL_W + (b + 1) * LANES]
        return jnp.concatenate([perm_scr[slot, b] for b in range(DIL_W // LANES)], axis=1)

    d1, d2 = DIL_PATTERNS[1][1], DIL_PATTERNS[2][1]
    l0, l1, l2 = l0_ref[...], natural(l1_ref, 0, d1), natural(l2_ref, 1, d2)
    mx = jnp.maximum(jnp.maximum(l0, l1), l2)
    e0, e1, e2 = jnp.exp2(l0 - mx), jnp.exp2(l1 - mx), jnp.exp2(l2 - mx)
    od = (e0 * o0_ref[...] + e1 * natural(o1_ref, 2, d1) + e2 * natural(o2_ref, 3, d2)) / (e0 + e1 + e2)
    o = jnp.concatenate([om_ref[...], od.astype(BF16)], axis=1)
    x1 = x_ref[...] + jnp.dot(o, wo_ref[...], preferred_element_type=F32)
    y_ref[...] = _ffn(x1, gf_ref[...], wgu_ref, wdown_ref)


def _out_ffn_rec_kernel(x_ref, a_ref, wo_ref, gf_ref, wgu_ref, wdown_ref, gfin_ref, y_ref):
    x1 = x_ref[...] + jnp.dot(a_ref[...], wo_ref[...], preferred_element_type=F32)
    y_ref[...] = _rms(_ffn(x1, gf_ref[...], wgu_ref, wdown_ref), gfin_ref[...])


def _out_ffn_attn(x, om, os_, ls_, wp, layer, tm):
    T, D = x.shape
    row = lambda w: pl.BlockSpec((tm, w), lambda i: (i, 0))
    grp = [pl.BlockSpec((tm // d, d * DIL_W), lambda i: (i, 0)) for _, d in DIL_PATTERNS]
    return pl.pallas_call(
        _out_ffn_attn_kernel,
        grid=(T // tm,),
        in_specs=[row(D), row(DIL_W)] + grp + grp + [
            _const_spec((MLA_HEADS * MLA_V + DIL_W, D)), _const_spec((1, D)),
            _const_spec((D, 2 * D_FF)), _const_spec((D_FF, D))],
        out_specs=row(D),
        out_shape=jax.ShapeDtypeStruct((T, D), F32),
        scratch_shapes=[pltpu.VMEM((4, DIL_W // LANES, tm, LANES), F32)],
        compiler_params=_cparams(("parallel",)),
    )(x, om, *os_, *ls_, wp["w_out_a"], wp["norm_ffn"][layer],
      wp["w_gu"][layer], wp["w_down"][layer])


def _out_ffn_rec(x, a, wp, layer, tm):
    T, D = x.shape
    row = lambda w: pl.BlockSpec((tm, w), lambda i: (i, 0))
    return pl.pallas_call(
        _out_ffn_rec_kernel,
        grid=(T // tm,),
        in_specs=[row(D), row(D_RNN), _const_spec((D_RNN, D)), _const_spec((1, D)),
                  _const_spec((D, 2 * D_FF)), _const_spec((D_FF, D)), _const_spec((1, D))],
        out_specs=row(D),
        out_shape=jax.ShapeDtypeStruct((T, D), F32),
        compiler_params=_cparams(("parallel",)),
    )(x, a, wp["w_out_r"], wp["norm_ffn"][layer], wp["w_gu"][layer], wp["w_down"][layer],
      wp["norm_final"])


def _l1_prep_kernel(x_ref, g_ref, w_ref, y_ref, xr_ref):
    h = _rms(x_ref[...], g_ref[...]).astype(BF16)
    zy = jnp.dot(h, w_ref[:, :D_RNN], preferred_element_type=F32)
    y_ref[...] = (0.5 * zy * (1.0 + jnp.tanh(np.sqrt(2.0 / np.pi).astype(np.float32)
                                             * (zy + 0.044715 * (zy * zy * zy))))).astype(BF16)
    xr_ref[...] = jnp.dot(h, w_ref[:, D_RNN:], preferred_element_type=F32)


def _l1_prep(x, g, w, tm):
    T, D = x.shape
    row = lambda w_: pl.BlockSpec((tm, w_), lambda i: (i, 0))
    return pl.pallas_call(
        _l1_prep_kernel,
        grid=(T // tm,),
        in_specs=[row(D), _const_spec((1, D)), _const_spec((D, 2 * D_RNN))],
        out_specs=[row(D_RNN), row(D_RNN)],
        out_shape=[jax.ShapeDtypeStruct((T, D_RNN), BF16), jax.ShapeDtypeStruct((T, D_RNN), F32)],
        compiler_params=_cparams(("parallel",)),
    )(x, g, w)


def _rglru_kernel(*refs, ns, ts, cb, reverse):
    if reverse:
        (xp_ref, xc_ref, xn_ref, cw_ref, cbias_ref, wg_ref, br_ref, bi_ref, lam_ref, hf_ref, y_ref,
         o_ref, a_scr, u_scr, h_scr, carry_scr) = refs
    else:
        (xp_ref, xc_ref, xn_ref, cw_ref, cbias_ref, wg_ref, br_ref, bi_ref, lam_ref,
         o_ref, a_scr, u_scr, carry_scr) = refs
        h_scr = o_ref.at[0]
    step = pl.program_id(2)
    tile = (ns - 1 - step) if reverse else step

    @pl.when(step == 0)
    def _():
        carry_scr[...] = jnp.zeros(carry_scr.shape, F32)

    x0 = xc_ref[0]
    rows = lax.broadcasted_iota(jnp.int32, (ts, cb), 0)
    pv = jnp.where(tile > 0, xp_ref[0], 0.0)
    nv = jnp.where(tile < ns - 1, xn_ref[0], 0.0)
    xm1 = jnp.where(rows == 0, pv[SUBLANES - 1:SUBLANES], pltpu.roll(x0, 1, 0))
    xm2 = pltpu.roll(x0, 2, 0)
    xm2 = jnp.where(rows == 0, pv[SUBLANES - 2:SUBLANES - 1], xm2)
    xm2 = jnp.where(rows == 1, pv[SUBLANES - 1:SUBLANES], xm2)
    xp1 = jnp.where(rows == ts - 1, nv[0:1], pltpu.roll(x0, ts - 1, 0))
    xc = (xm2 * cw_ref[0:1] + xm1 * cw_ref[1:2] + x0 * cw_ref[2:3] + xp1 * cw_ref[3:4]) + cbias_ref[...]

    xcb = xc.astype(BF16)
    for j in range(cb // LRU_BW):
        sl = slice(j * LRU_BW, (j + 1) * LRU_BW)
        gts = jnp.dot(xcb[:, sl], wg_ref[j], preferred_element_type=F32)
        r = jax.nn.sigmoid(gts[:, :LRU_BW] + br_ref[:, sl])
        ig = jax.nn.sigmoid(gts[:, LRU_BW:] + bi_ref[:, sl])
        nlam = -lam_ref[:, sl]
        softplus = jnp.maximum(nlam, 0.0) + jnp.log1p(jnp.exp(-jnp.abs(nlam)))
        a = jnp.exp((-LRU_C) * r * softplus)
        a_scr[:, sl] = a
        u_scr[:, sl] = jnp.sqrt(jnp.maximum(1.0 - a * a, 0.0)) * (ig * xc[:, sl])

    row8 = lax.broadcasted_iota(jnp.int32, (SUBLANES, cb), 0)
    nchunk = ts // SUBLANES

    def chunk(c, hprev):
        idx = (nchunk - 1 - c) if reverse else c
        r0 = pl.multiple_of(idx * SUBLANES, SUBLANES)
        a = a_scr[pl.ds(r0, SUBLANES), :]
        b = u_scr[pl.ds(r0, SUBLANES), :]
        for d in (1, 2, 4):
            if reverse:
                keep = row8 < SUBLANES - d
                sh = SUBLANES - d
            else:
                keep = row8 >= d
                sh = d
            a_s = jnp.where(keep, pltpu.roll(a, sh, 0), 1.0)
            b_s = jnp.where(keep, pltpu.roll(b, sh, 0), 0.0)
            b = a * b_s + b
            a = a * a_s
        h = a * hprev + b
        h_scr[pl.ds(r0, SUBLANES), :] = h
        edge = h[0:1] if reverse else h[SUBLANES - 1:SUBLANES]
        return jnp.broadcast_to(edge, (SUBLANES, cb))

    carry_scr[...] = lax.fori_loop(0, nchunk, chunk, carry_scr[...], unroll=4)

    if reverse:
        o_ref[0] = ((hf_ref[0] + h_scr[...]) * y_ref[0].astype(F32)).astype(BF16)


def _rglru(xr, wp, ts, cb, reverse, hf=None, y=None):
    B, S, C = xr.shape
    ns = S // ts
    nhb = S // SUBLANES
    hb = ts // SUBLANES
    d = 1 if reverse else 0
    tidx = (lambda i: ns - 1 - i) if reverse else (lambda i: i)
    cur = pl.BlockSpec((1, ts, cb), lambda b, c, i: (b, tidx(i), c))
    prev = pl.BlockSpec((1, SUBLANES, cb), lambda b, c, i: (b, jnp.maximum(tidx(i) * hb - 1, 0), c))
    nxt = pl.BlockSpec((1, SUBLANES, cb), lambda b, c, i: (b, jnp.minimum((tidx(i) + 1) * hb, nhb - 1), c))
    vec = pl.BlockSpec((1, cb), lambda b, c, i: (0, c))
    in_specs = [prev, cur, nxt,
                pl.BlockSpec((CONV_W, cb), lambda b, c, i: (0, c)), vec,
                pl.BlockSpec((cb // LRU_BW, LRU_BW, 2 * LRU_BW), lambda b, c, i: (c, 0, 0)),
                vec, vec, vec]
    args = [xr, xr, xr, wp["conv_w"], wp["conv_b"], wp["w_gate"][d], wp["b_r"][d], wp["b_i"][d],
            wp["lam"][d]]
    scratch = [pltpu.VMEM((ts, cb), F32), pltpu.VMEM((ts, cb), F32)]
    if reverse:
        in_specs += [cur, cur]
        args += [hf, y]
        scratch.append(pltpu.VMEM((ts, cb), F32))
        out_dtype = BF16
    else:
        out_dtype = F32
    scratch.append(pltpu.VMEM((SUBLANES, cb), F32))
    return pl.pallas_call(
        functools.partial(_rglru_kernel, ns=ns, ts=ts, cb=cb, reverse=reverse),
        grid=(B, C // cb, ns),
        in_specs=in_specs,
        out_specs=cur,
        out_shape=jax.ShapeDtypeStruct((B, S, C), out_dtype),
        scratch_shapes=scratch,
        compiler_params=_cparams(("parallel", "parallel", "arbitrary")),
    )(*args)


def _lane_table(vals, jidx, mask, fill):
    return jnp.where(mask[None, :], vals[:, jidx], fill)


def _rope_tables(S, half, period, start, scale):
    inv = jnp.power(ROPE_THETA, -jnp.arange(half, dtype=F32) / half)
    ang = jnp.arange(S, dtype=F32)[:, None] * inv[None, :]
    cos, sin = jnp.cos(ang), jnp.sin(ang)
    e = np.arange(LANES) % period - start
    first = (e >= 0) & (e < half)
    second = (e >= half) & (e < 2 * half)
    jidx = np.where(first | second, e % half, 0)
    c = _lane_table(cos, jidx, first | second, 1.0)
    s1 = _lane_table(-sin, jidx, first, 0.0)
    s2 = _lane_table(sin, jidx, second, 0.0)
    return jnp.stack([c, s1, s2]) * scale


def _all_tables(S):
    mla_scale = float((MLA_NOPE + MLA_ROPE) ** -0.5 * np.log2(np.e))
    dil_scale = float(HEAD_DIM ** -0.5 * np.log2(np.e))
    return jnp.concatenate([
        _rope_tables(S, MLA_ROPE // 2, MLA_PAD, MLA_NOPE, mla_scale),
        _rope_tables(S, MLA_ROPE // 2, MLA_PAD, MLA_NOPE, 1.0),
        _rope_tables(S, DIL_ROT // 2, HEAD_DIM, 0, dil_scale),
        _rope_tables(S, DIL_ROT // 2, HEAD_DIM, 0, 1.0),
    ])


def _prep_weights(norm_mix, w_in_a, q_norm, w_uq, kv_norm, w_ukv, w_out_a, w_in_r, conv_w, conv_b,
                  lru_w_gate, lru_b_gate, lru_lambda, w_out_r, norm_ffn, w_gu, w_down, norm_final):
    w_in = w_in_a[0]
    qk_dim = MLA_NOPE + MLA_ROPE
    kr_cols = jnp.pad(w_in[:, MLA_Q_RANK + MLA_KV_RANK:MLA_IN],
                      ((0, 0), (MLA_NOPE, MLA_PAD - qk_dim)))
    w_mla = jnp.concatenate([w_in[:, :MLA_Q_RANK + MLA_KV_RANK], kr_cols], axis=1)
    uq = w_uq[0].reshape(MLA_Q_RANK, MLA_HEADS, qk_dim)
    uq = jnp.pad(uq, ((0, 0), (0, 0), (0, MLA_PAD - qk_dim))).reshape(MLA_Q_RANK, MLA_HEADS * MLA_PAD)
    ukv = w_ukv[0].reshape(MLA_KV_RANK, MLA_HEADS, MLA_NOPE + MLA_V)
    uk = jnp.pad(ukv[:, :, :MLA_NOPE], ((0, 0), (0, 0), (0, MLA_PAD - MLA_NOPE)))
    uk = uk.reshape(MLA_KV_RANK, MLA_HEADS * MLA_PAD)
    uv = ukv[:, :, MLA_NOPE:].reshape(MLA_KV_RANK, MLA_HEADS * MLA_V).T
    wg = lru_w_gate[0]
    w_gate = jnp.concatenate([wg[:, 0], wg[:, 1]], axis=-1)
    row = lambda v: v.reshape(1, -1)
    return {
        "norm_mix": [row(norm_mix[l]) for l in range(2)],
        "w_mla": w_mla.astype(BF16),
        "w_dil": w_in[:, MLA_IN:].astype(BF16),
        "q_norm": row(q_norm[0]),
        "w_uq": uq.astype(BF16),
        "kv_norm": row(kv_norm[0]),
        "w_uk": uk.astype(BF16),
        "w_uv": uv.astype(BF16),
        "w_out_a": w_out_a[0].astype(BF16),
        "w_in_r": w_in_r[0].astype(BF16),
        "conv_w": conv_w[0],
        "conv_b": row(conv_b[0]),
        "w_gate": w_gate.astype(BF16),
        "b_r": [row(lru_b_gate[0, d, 0]) for d in range(2)],
        "b_i": [row(lru_b_gate[0, d, 1]) for d in range(2)],
        "lam": [row(lru_lambda[0, d]) for d in range(2)],
        "w_out_r": w_out_r[0].astype(BF16),
        "norm_ffn": [row(norm_ffn[l]) for l in range(2)],
        "w_gu": [w_gu[l].astype(BF16) for l in range(2)],
        "w_down": [w_down[l].astype(BF16) for l in range(2)],
        "norm_final": row(norm_final),
    }


def _trunk(x, wp):
    B, S, D = x.shape
    T = B * S
    tabs = _all_tables(S)
    q, k, vt, *dgs = _l0_prep(x, wp["norm_mix"][0], wp, tabs, tm=512)
    o_mla = _mla_attention(q, k, vt, tq=min(S, 2048), tk=min(S, 2048))
    os_, ls_ = [], []
    for dg, (_, dil) in zip(dgs, DIL_PATTERNS):
        o_g, l_g = _dil_attention(dg, dil, tt=min(S // dil, 512))
        os_.append(o_g.reshape(T // dil, dil * DIL_W))
        ls_.append(l_g.reshape(T // dil, dil * DIL_W))
    x1 = _out_ffn_attn(x.reshape(T, D), o_mla.reshape(T, MLA_HEADS * MLA_V), os_, ls_, wp, 0, tm=512)
    y, xr = _l1_prep(x1, wp["norm_mix"][1], wp["w_in_r"], tm=512)
    xr = xr.reshape(B, S, D_RNN)
    hf = _rglru(xr, wp, ts=512, cb=D_RNN, reverse=False)
    a = _rglru(xr, wp, ts=512, cb=D_RNN, reverse=True, hf=hf, y=y.reshape(B, S, D_RNN))
    out = _out_ffn_rec(x1, a.reshape(T, D_RNN), wp, 1, tm=512)
    return out.reshape(B, S, D)


def kernel(x_prompt, x_sample, norm_mix, w_in_a, q_norm, w_uq, kv_norm, w_ukv, w_out_a, w_in_r, conv_w,
           conv_b, lru_w_gate, lru_b_gate, lru_lambda, w_out_r, norm_ffn, w_gu, w_down, norm_final):
    wp = _prep_weights(norm_mix, w_in_a, q_norm, w_uq, kv_norm, w_ukv, w_out_a, w_in_r, conv_w, conv_b,
                       lru_w_gate, lru_b_gate, lru_lambda, w_out_r, norm_ffn, w_gu, w_down, norm_final)
    return (_trunk(x_prompt, wp), _trunk(x_sample, wp))
```

```python
import functools

import numpy as np
import jax
import jax.numpy as jnp
from jax import lax
from jax.experimental import pallas as pl
from jax.experimental.pallas import tpu as pltpu

F32 = jnp.float32
BF16 = jnp.bfloat16

D_MODEL = 1024
HEAD_DIM = 64
ROPE_THETA = 500000.0
NORM_EPS = 1e-6
MLA_HEADS = 8
MLA_NOPE = 64
MLA_ROPE = 32
MLA_V = 64
MLA_Q_RANK = 256
MLA_KV_RANK = 128
DIL_PATTERNS = ((128, 1), (512, 4), (2048, 16))
DIL_GROUPS = len(DIL_PATTERNS)
DIL_HEADS = 8
DIL_ROT = HEAD_DIM // 4
DIL_STEPS = 64
DIL_QB = 128
DIL_ROWS_PER_STEP = 512
MLA_IN = MLA_Q_RANK + MLA_KV_RANK + MLA_ROPE
DIL_W = DIL_HEADS * HEAD_DIM
DIL_QKV = 3 * DIL_GROUPS * DIL_W
D_RNN = 1536
LRU_BLOCKS = 12
LRU_BW = D_RNN // LRU_BLOCKS
CONV_W = 4
CONV_LEFT = 2
LRU_C = 8.0
D_FF = ((8 * D_MODEL // 3 + 255) // 256) * 256
NEG_BIG = -1e30

LANES = 128
SUBLANES = 8
MLA_PAD = 128
VMEM_LIMIT = 56 * 1024 * 1024

FF_CHUNK = 256
FF_CHUNKS = D_FF // FF_CHUNK


def _cparams(sem):
    return pltpu.CompilerParams(dimension_semantics=sem, vmem_limit_bytes=VMEM_LIMIT)


def _const_spec(shape):
    nd = len(shape)
    return pl.BlockSpec(shape, lambda *_: (0,) * nd, pipeline_mode=pl.Buffered(1))


def _rms(x, g):
    return x * lax.rsqrt(jnp.mean(x * x, axis=-1, keepdims=True) + NORM_EPS) * g


def _rope_block(xb, c, s1, s2, half):
    return xb * c + pltpu.roll(xb, LANES - half, 1) * s1 + pltpu.roll(xb, half, 1) * s2


def _l0_prep_kernel(x_ref, g_ref, wmla_ref, wdil_ref, qn_ref, wuq_ref, kvn_ref, wuk_ref, wuv_ref,
                    tab_ref, q_out, k_out, v_out, d0_out, d1_out, d2_out, perm_scr):
    tm = x_ref.shape[1]
    d_outs = (d0_out, d1_out, d2_out)
    x = x_ref[0]
    h = _rms(x, g_ref[...]).astype(BF16)
    z = jnp.dot(h, wmla_ref[...], preferred_element_type=F32)
    qn = _rms(z[:, :MLA_Q_RANK], qn_ref[...]).astype(BF16)
    kvn = _rms(z[:, MLA_Q_RANK:MLA_Q_RANK + MLA_KV_RANK], kvn_ref[...]).astype(BF16)
    kr = z[:, MLA_Q_RANK + MLA_KV_RANK:]
    qf = jnp.dot(qn, wuq_ref[...], preferred_element_type=F32)
    kf = jnp.dot(kvn, wuk_ref[...], preferred_element_type=F32)
    vt = lax.dot_general(wuv_ref[...], kvn, (((1,), (1,)), ((), ())),
                         preferred_element_type=F32)
    half = MLA_ROPE // 2
    krr = _rope_block(kr, tab_ref[3], tab_ref[4], tab_ref[5], half)
    for hh in range(MLA_HEADS):
        sl = slice(hh * MLA_PAD, (hh + 1) * MLA_PAD)
        q_out[0, :, sl] = _rope_block(qf[:, sl], tab_ref[0], tab_ref[1], tab_ref[2], half).astype(BF16)
        k_out[0, :, sl] = (kf[:, sl] + krr).astype(BF16)
    v_out[0] = vt.astype(BF16)
    dhalf = DIL_ROT // 2
    for j in range(3 * DIL_GROUPS):
        c, g = divmod(j, DIL_GROUPS)
        dil = DIL_PATTERNS[g][1]
        zc = jnp.dot(h, wdil_ref[:, j * DIL_W:(j + 1) * DIL_W], preferred_element_type=F32)
        for b in range(DIL_W // LANES):
            zb = zc[:, b * LANES:(b + 1) * LANES]
            if c < 2:
                t0 = 6 + 3 * c
                zb = _rope_block(zb, tab_ref[t0], tab_ref[t0 + 1], tab_ref[t0 + 2], dhalf)
            if dil == 1:
                d_outs[g][0, :, c * DIL_W + b * LANES:c * DIL_W + (b + 1) * LANES] = zb.astype(BF16)
            else:
                perm_scr[b] = zb
        if dil > 1:
            for r in range(dil):
                for b in range(DIL_W // LANES):
                    col = (r * 3 + c) * DIL_W + b * LANES
                    d_outs[g][0, :, col:col + LANES] = perm_scr[
                        b, pl.ds(r, tm // dil, stride=dil), :].astype(BF16)


def _l0_prep(x, g, wp, tabs, tm):
    B, S, D = x.shape
    grid = (B, S // tm)
    row = lambda w: pl.BlockSpec((1, tm, w), lambda b, i: (b, i, 0))
    return pl.pallas_call(
        _l0_prep_kernel,
        grid=grid,
        in_specs=[
            row(D),
            _const_spec((1, D)),
            _const_spec(wp["w_mla"].shape),
            _const_spec(wp["w_dil"].shape),
            _const_spec((1, MLA_Q_RANK)),
            _const_spec(wp["w_uq"].shape),
            _const_spec((1, MLA_KV_RANK)),
            _const_spec(wp["w_uk"].shape),
            _const_spec(wp["w_uv"].shape),
            pl.BlockSpec((12, tm, LANES), lambda b, i: (0, i, 0)),
        ],
        out_specs=[row(MLA_HEADS * MLA_PAD), row(MLA_HEADS * MLA_PAD),
                   pl.BlockSpec((1, MLA_HEADS * MLA_V, tm), lambda b, i: (b, 0, i))] + [
            pl.BlockSpec((1, tm // d, d * 3 * DIL_W), lambda b, i: (b, i, 0)) for _, d in DIL_PATTERNS],
        out_shape=[
            jax.ShapeDtypeStruct((B, S, MLA_HEADS * MLA_PAD), BF16),
            jax.ShapeDtypeStruct((B, S, MLA_HEADS * MLA_PAD), BF16),
            jax.ShapeDtypeStruct((B, MLA_HEADS * MLA_V, S), BF16),
        ] + [jax.ShapeDtypeStruct((B, S // d, d * 3 * DIL_W), BF16) for _, d in DIL_PATTERNS],
        scratch_shapes=[pltpu.VMEM((DIL_W // LANES, tm, LANES), F32)],
        compiler_params=_cparams(("parallel", "parallel")),
    )(x, g, wp["w_mla"], wp["w_dil"], wp["q_norm"], wp["w_uq"], wp["kv_norm"], wp["w_uk"], wp["w_uv"], tabs)


def _mla_kernel(q_ref, k_ref, vt_ref, o_ref, m_scr, acc_scr, *, nk, cq):
    ki = pl.program_id(3)

    @pl.when(ki == 0)
    def _():
        m_scr[...] = jnp.full(m_scr.shape, -jnp.inf, F32)
        acc_scr[...] = jnp.zeros(acc_scr.shape, F32)

    tq = q_ref.shape[1]
    hps = m_scr.shape[0]
    pair_shape = (2 * MLA_V, vt_ref.shape[2])
    own_rows = lax.broadcasted_iota(jnp.int32, pair_shape, 0) < MLA_V
    ones = jnp.ones(pair_shape, BF16)
    v1 = []
    for pair in range(hps // 2):
        vt = vt_ref[0, pair * 2 * MLA_V:(pair + 1) * 2 * MLA_V, :]
        v1 += [jnp.where(own_rows, vt, ones), jnp.where(own_rows, ones, vt)]

    def scores(hh, c):
        q = q_ref[0, c * cq:(c + 1) * cq, hh * MLA_PAD:(hh + 1) * MLA_PAD]
        k = k_ref[0, :, hh * MLA_PAD:(hh + 1) * MLA_PAD]
        st = lax.dot_general(k, q, (((1,), (1,)), ((), ())), preferred_element_type=F32)
        m_prev = m_scr[hh, :, c * cq:(c + 1) * cq]
        m_new = jnp.maximum(m_prev, jnp.max(st, axis=0, keepdims=True))
        m_scr[hh, :, c * cq:(c + 1) * cq] = m_new
        return st, m_prev, m_new

    def accumulate(hh, c, st, m_prev, m_new):
        alpha = jnp.exp2(m_prev - m_new)
        p = jnp.exp2(st - m_new).astype(BF16)
        acc_scr[hh, :, c * cq:(c + 1) * cq] = alpha * acc_scr[hh, :, c * cq:(c + 1) * cq] + jnp.dot(
            v1[hh], p, preferred_element_type=F32)

    tasks = [(hh, c) for hh in range(hps) for c in range(tq // cq)]
    pending = scores(*tasks[0])
    for t, task in enumerate(tasks):
        nxt = scores(*tasks[t + 1]) if t + 1 < len(tasks) else None
        accumulate(*task, *pending)
        pending = nxt

    @pl.when(ki == nk - 1)
    def _():
        for pair in range(hps // 2):
            a0, a1 = acc_scr[2 * pair], acc_scr[2 * pair + 1]
            ot = jnp.concatenate([a0[:MLA_V] / a0[MLA_V:MLA_V + 1], a1[MLA_V:] / a1[0:1]], axis=0)
            o_ref[0, :, pair * 2 * MLA_V:(pair + 1) * 2 * MLA_V] = ot.T.astype(BF16)


def _mla_attention(q, k, vt, tq, tk, cq=1024, hps=2):
    B, S, _ = q.shape
    nq, nk = S // tq, S // tk
    return pl.pallas_call(
        functools.partial(_mla_kernel, nk=nk, cq=cq),
        grid=(B, MLA_HEADS // hps, nq, nk),
        in_specs=[
            pl.BlockSpec((1, tq, hps * MLA_PAD), lambda b, h, i, j: (b, i, h)),
            pl.BlockSpec((1, tk, hps * MLA_PAD), lambda b, h, i, j: (b, j, h)),
            pl.BlockSpec((1, hps * MLA_V, tk), lambda b, h, i, j: (b, h, j)),
        ],
        out_specs=pl.BlockSpec((1, tq, hps * MLA_V), lambda b, h, i, j: (b, i, h)),
        out_shape=jax.ShapeDtypeStruct((B, S, MLA_HEADS * MLA_V), BF16),
        scratch_shapes=[
            pltpu.VMEM((hps, 1, tq), F32),
            pltpu.VMEM((hps, 2 * MLA_V, tq), F32),
        ],
        compiler_params=_cparams(("parallel", "parallel", "parallel", "arbitrary")),
    )(q, k, vt)


def _dil_kernel(prev_ref, cur_ref, next_ref, o_ref, lse_ref, *, nt, tt, nres):
    i = pl.program_id(2)
    nsub = tt // DIL_QB
    nkx = DIL_QB + 2 * DIL_STEPS
    qi = lax.broadcasted_iota(jnp.int32, (DIL_QB, nkx), 0)
    kx = lax.broadcasted_iota(jnp.int32, (DIL_QB, nkx), 1)
    rel = kx - DIL_STEPS - qi
    band = (rel >= -DIL_STEPS) & (rel <= DIL_STEPS)
    lane_q = lax.broadcasted_iota(jnp.int32, (DIL_QB, LANES), 1)
    first = lane_q < HEAD_DIM
    zero = jnp.zeros((DIL_QB, LANES), BF16)
    for res in range(nres):
        qc, kc, vc = (res * 3 * DIL_W + c * DIL_W for c in range(3))
        kext = jnp.concatenate([r[0, :, kc:kc + DIL_W] for r in (prev_ref, cur_ref, next_ref)], axis=0)
        vext = jnp.concatenate([r[0, :, vc:vc + DIL_W] for r in (prev_ref, cur_ref, next_ref)], axis=0)
        for j in range(nsub):
            ok = band
            if j == 0:
                ok = ok & ((kx >= DIL_STEPS) | (i > 0))
            if j == nsub - 1:
                ok = ok & ((kx < DIL_QB + DIL_STEPS) | (i < nt - 1))
            rows = slice(j * DIL_QB, (j + 1) * DIL_QB)
            krows = slice(j * DIL_QB, j * DIL_QB + nkx)
            for hp in range(DIL_HEADS // 2):
                sl = slice(hp * LANES, (hp + 1) * LANES)
                qp = cur_ref[0, rows, qc + hp * LANES:qc + (hp + 1) * LANES]
                kp = kext[krows, sl]
                vp = vext[krows, sl]
                outs, lses = [], []
                for hh in range(2):
                    qh = jnp.where(first if hh == 0 else ~first, qp, zero)
                    s = lax.dot_general(qh, kp, (((1,), (1,)), ((), ())), preferred_element_type=F32)
                    s = jnp.where(ok, s, NEG_BIG)
                    m = jnp.max(s, axis=1, keepdims=True)
                    e = jnp.exp2(s - m)
                    den = jnp.sum(e, axis=1, keepdims=True)
                    outs.append(jnp.dot(e.astype(BF16), vp, preferred_element_type=F32) / den)
                    lses.append(jnp.broadcast_to(m + jnp.log2(den), (DIL_QB, LANES)))
                oc = res * DIL_W + hp * LANES
                o_ref[0, rows, oc:oc + LANES] = jnp.where(first, outs[0], outs[1])
                lse_ref[0, rows, oc:oc + LANES] = jnp.where(first, lses[0], lses[1])


def _dil_attention(x, dil, tt):
    B, L, _ = x.shape
    nt = L // tt
    nres = max(1, min(dil, DIL_ROWS_PER_STEP // tt))
    hb = tt // DIL_STEPS
    nhb = L // DIL_STEPS
    w_in, w_out = nres * 3 * DIL_W, nres * DIL_W
    cur = pl.BlockSpec((1, tt, w_in), lambda b, r, i: (b, i, r))
    prev = pl.BlockSpec((1, DIL_STEPS, w_in), lambda b, r, i: (b, jnp.maximum(i * hb - 1, 0), r))
    nxt = pl.BlockSpec((1, DIL_STEPS, w_in), lambda b, r, i: (b, jnp.minimum((i + 1) * hb, nhb - 1), r))
    out_spec = pl.BlockSpec((1, tt, w_out), lambda b, r, i: (b, i, r))
    return pl.pallas_call(
        functools.partial(_dil_kernel, nt=nt, tt=tt, nres=nres),
        grid=(B, dil // nres, nt),
        in_specs=[prev, cur, nxt],
        out_specs=[out_spec, out_spec],
        out_shape=[jax.ShapeDtypeStruct((B, L, dil * DIL_W), F32)] * 2,
        compiler_params=_cparams(("parallel", "parallel", "parallel")),
    )(x, x, x)


def _ffn(x1, gf, wgu_ref, wdown_ref):
    h2 = _rms(x1, gf).astype(BF16)
    acc = x1
    for c in range(FF_CHUNKS):
        gate = jnp.dot(h2, wgu_ref[:, c * FF_CHUNK:(c + 1) * FF_CHUNK], preferred_element_type=F32)
        up = jnp.dot(h2, wgu_ref[:, D_FF + c * FF_CHUNK:D_FF + (c + 1) * FF_CHUNK],
                     preferred_element_type=F32)
        act = (gate * jax.nn.sigmoid(gate) * up).astype(BF16)
        acc = acc + jnp.dot(act, wdown_ref[c * FF_CHUNK:(c + 1) * FF_CHUNK, :], preferred_element_type=F32)
    return acc


def _out_ffn_attn_kernel(x_ref, om_ref, o0_ref, o1_ref, o2_ref, l0_ref, l1_ref, l2_ref,
                         wo_ref, gf_ref, wgu_ref, wdown_ref, y_ref, perm_scr):
    tm = x_ref.shape[0]

    def natural(ref, slot, dil):
        for r in range(dil):
            for b in range(DIL_W // LANES):
                perm_scr[slot, b, pl.ds(r, tm // dil, stride=dil), :] = ref[
                    :, r * DIL_W + b * LANES:r * DIL_W + (b + 1) * LANES]
        return jnp.concatenate([perm_scr[slot, b] for b in range(DIL_W // LANES)], axis=1)

    d1, d2 = DIL_PATTERNS[1][1], DIL_PATTERNS[2][1]
    l0, l1, l2 = l0_ref[...], natural(l1_ref, 0, d1), natural(l2_ref, 1, d2)
    mx = jnp.maximum(jnp.maximum(l0, l1), l2)
    e0, e1, e2 = jnp.exp2(l0 - mx), jnp.exp2(l1 - mx), jnp.exp2(l2 - mx)
    od = (e0 * o0_ref[...] + e1 * natural(o1_ref, 2, d1) + e2 * natural(o2_ref, 3, d2)) / (e0 + e1 + e2)
    o = jnp.concatenate([om_ref[...], od.astype(BF16)], axis=1)
    x1 = x_ref[...] + jnp.dot(o, wo_ref[...], preferred_element_type=F32)
    y_ref[...] = _ffn(x1, gf_ref[...], wgu_ref, wdown_ref)


def _out_ffn_rec_kernel(x_ref, a_ref, wo_ref, gf_ref, wgu_ref, wdown_ref, gfin_ref, y_ref):
    x1 = x_ref[...] + jnp.dot(a_ref[...], wo_ref[...], preferred_element_type=F32)
    y_ref[...] = _rms(_ffn(x1, gf_ref[...], wgu_ref, wdown_ref), gfin_ref[...])


def _out_ffn_attn(x, om, os_, ls_, wp, layer, tm):
    T, D = x.shape
    row = lambda w: pl.BlockSpec((tm, w), lambda i: (i, 0))
    grp = [pl.BlockSpec((tm // d, d * DIL_W), lambda i: (i, 0)) for _, d in DIL_PATTERNS]
    return pl.pallas_call(
        _out_ffn_attn_kernel,
        grid=(T // tm,),
        in_specs=[row(D), row(DIL_W)] + grp + grp + [
            _const_spec((MLA_HEADS * MLA_V + DIL_W, D)), _const_spec((1, D)),
            _const_spec((D, 2 * D_FF)), _const_spec((D_FF, D))],
        out_specs=row(D),
        out_shape=jax.ShapeDtypeStruct((T, D), F32),
        scratch_shapes=[pltpu.VMEM((4, DIL_W // LANES, tm, LANES), F32)],
        compiler_params=_cparams(("parallel",)),
    )(x, om, *os_, *ls_, wp["w_out_a"], wp["norm_ffn"][layer],
      wp["w_gu"][layer], wp["w_down"][layer])


def _out_ffn_rec(x, a, wp, layer, tm):
    T, D = x.shape
    row = lambda w: pl.BlockSpec((tm, w), lambda i: (i, 0))
    return pl.pallas_call(
        _out_ffn_rec_kernel,
        grid=(T // tm,),
        in_specs=[row(D), row(D_RNN), _const_spec((D_RNN, D)), _const_spec((1, D)),
                  _const_spec((D, 2 * D_FF)), _const_spec((D_FF, D)), _const_spec((1, D))],
        out_specs=row(D),
        out_shape=jax.ShapeDtypeStruct((T, D), F32),
        compiler_params=_cparams(("parallel",)),
    )(x, a, wp["w_out_r"], wp["norm_ffn"][layer], wp["w_gu"][layer], wp["w_down"][layer],
      wp["norm_final"])


def _l1_prep_kernel(x_ref, g_ref, w_ref, y_ref, xr_ref):
    h = _rms(x_ref[...], g_ref[...]).astype(BF16)
    zy = jnp.dot(h, w_ref[:, :D_RNN], preferred_element_type=F32)
    y_ref[...] = (0.5 * zy * (1.0 + jnp.tanh(np.sqrt(2.0 / np.pi).astype(np.float32)
                                             * (zy + 0.044715 * (zy * zy * zy))))).astype(BF16)
    xr_ref[...] = jnp.dot(h, w_ref[:, D_RNN:], preferred_element_type=F32)


def _l1_prep(x, g, w, tm):
    T, D = x.shape
    row = lambda w_: pl.BlockSpec((tm, w_), lambda i: (i, 0))
    return pl.pallas_call(
        _l1_prep_kernel,
        grid=(T // tm,),
        in_specs=[row(D), _const_spec((1, D)), _const_spec((D, 2 * D_RNN))],
        out_specs=[row(D_RNN), row(D_RNN)],
        out_shape=[jax.ShapeDtypeStruct((T, D_RNN), BF16), jax.ShapeDtypeStruct((T, D_RNN), F32)],
        compiler_params=_cparams(("parallel",)),
    )(x, g, w)


def _rglru_kernel(*refs, ns, ts, cb, reverse):
    if reverse:
        (xconv_ref, wg_ref, br_ref, bi_ref, lam_ref, hf_ref, y_ref,
         o_ref, a_scr, u_scr, h_scr, carry_scr) = refs
    else:
        (xp_ref, xc_ref, xn_ref, cw_ref, cbias_ref, wg_ref, br_ref, bi_ref, lam_ref,
         o_ref, xconv_ref, a_scr, u_scr, carry_scr) = refs
        h_scr = o_ref.at[0]
    step = pl.program_id(2)

    @pl.when(step == 0)
    def _():
        carry_scr[...] = jnp.zeros(carry_scr.shape, F32)

    if reverse:
        xc = xconv_ref[0]
    else:
        x0 = xc_ref[0]
        rows = lax.broadcasted_iota(jnp.int32, (ts, cb), 0)
        pv = jnp.where(step > 0, xp_ref[0], 0.0)
        nv = jnp.where(step < ns - 1, xn_ref[0], 0.0)
        xm1 = jnp.where(rows == 0, pv[SUBLANES - 1:SUBLANES], pltpu.roll(x0, 1, 0))
        xm2 = pltpu.roll(x0, 2, 0)
        xm2 = jnp.where(rows == 0, pv[SUBLANES - 2:SUBLANES - 1], xm2)
        xm2 = jnp.where(rows == 1, pv[SUBLANES - 1:SUBLANES], xm2)
        xp1 = jnp.where(rows == ts - 1, nv[0:1], pltpu.roll(x0, ts - 1, 0))
        xc = (xm2 * cw_ref[0:1] + xm1 * cw_ref[1:2] + x0 * cw_ref[2:3] + xp1 * cw_ref[3:4]) + cbias_ref[...]
        xconv_ref[0] = xc

    xcb = xc.astype(BF16)
    for j in range(cb // LRU_BW):
        sl = slice(j * LRU_BW, (j + 1) * LRU_BW)
        gts = jnp.dot(xcb[:, sl], wg_ref[j], preferred_element_type=F32)
        r = jax.nn.sigmoid(gts[:, :LRU_BW] + br_ref[:, sl])
        ig = jax.nn.sigmoid(gts[:, LRU_BW:] + bi_ref[:, sl])
        nlam = -lam_ref[:, sl]
        softplus = jnp.maximum(nlam, 0.0) + jnp.log1p(jnp.exp(-jnp.abs(nlam)))
        a = jnp.exp((-LRU_C) * r * softplus)
        a_scr[:, sl] = a
        u_scr[:, sl] = jnp.sqrt(jnp.maximum(1.0 - a * a, 0.0)) * (ig * xc[:, sl])

    row8 = lax.broadcasted_iota(jnp.int32, (SUBLANES, cb), 0)
    nchunk = ts // SUBLANES

    def chunk(c, hprev):
        idx = (nchunk - 1 - c) if reverse else c
        r0 = pl.multiple_of(idx * SUBLANES, SUBLANES)
        a = a_scr[pl.ds(r0, SUBLANES), :]
        b = u_scr[pl.ds(r0, SUBLANES), :]
        for d in (1, 2, 4):
            if reverse:
                keep = row8 < SUBLANES - d
                sh = SUBLANES - d
            else:
                keep = row8 >= d
                sh = d
            a_s = jnp.where(keep, pltpu.roll(a, sh, 0), 1.0)
            b_s = jnp.where(keep, pltpu.roll(b, sh, 0), 0.0)
            b = a * b_s + b
            a = a * a_s
        h = a * hprev + b
        h_scr[pl.ds(r0, SUBLANES), :] = h
        edge = h[0:1] if reverse else h[SUBLANES - 1:SUBLANES]
        return jnp.broadcast_to(edge, (SUBLANES, cb))

    carry_scr[...] = lax.fori_loop(0, nchunk, chunk, carry_scr[...], unroll=4)

    if reverse:
        o_ref[0] = ((hf_ref[0] + h_scr[...]) * y_ref[0].astype(F32)).astype(BF16)


def _rglru(xr, wp, ts, cb, reverse, hf=None, y=None):
    B, S, C = xr.shape
    ns = S // ts
    nhb = S // SUBLANES
    hb = ts // SUBLANES
    d = 1 if reverse else 0
    tidx = (lambda i: ns - 1 - i) if reverse else (lambda i: i)
    cur = pl.BlockSpec((1, ts, cb), lambda b, c, i: (b, tidx(i), c))
    vec = pl.BlockSpec((1, cb), lambda b, c, i: (0, c))
    gate_specs = [pl.BlockSpec((cb // LRU_BW, LRU_BW, 2 * LRU_BW), lambda b, c, i: (c, 0, 0)), vec, vec, vec]
    gate_args = [wp["w_gate"][d], wp["b_r"][d], wp["b_i"][d], wp["lam"][d]]
    tile = pltpu.VMEM((ts, cb), F32)
    carry = pltpu.VMEM((SUBLANES, cb), F32)
    if reverse:
        in_specs = [cur] + gate_specs + [cur, cur]
        args = [xr] + gate_args + [hf, y]
        scratch = [tile, tile, tile, carry]
        out_specs, out_shape = cur, jax.ShapeDtypeStruct((B, S, C), BF16)
    else:
        prev = pl.BlockSpec((1, SUBLANES, cb), lambda b, c, i: (b, jnp.maximum(i * hb - 1, 0), c))
        nxt = pl.BlockSpec((1, SUBLANES, cb), lambda b, c, i: (b, jnp.minimum((i + 1) * hb, nhb - 1), c))
        in_specs = [prev, cur, nxt, pl.BlockSpec((CONV_W, cb), lambda b, c, i: (0, c)), vec] + gate_specs
        args = [xr, xr, xr, wp["conv_w"], wp["conv_b"]] + gate_args
        scratch = [tile, tile, carry]
        out_specs, out_shape = [cur, cur], [jax.ShapeDtypeStruct((B, S, C), F32)] * 2
    return pl.pallas_call(
        functools.partial(_rglru_kernel, ns=ns, ts=ts, cb=cb, reverse=reverse),
        grid=(B, C // cb, ns),
        in_specs=in_specs,
        out_specs=out_specs,
        out_shape=out_shape,
        scratch_shapes=scratch,
        compiler_params=_cparams(("parallel", "parallel", "arbitrary")),
    )(*args)


def _lane_table(vals, jidx, mask, fill):
    return jnp.where(mask[None, :], vals[:, jidx], fill)


def _rope_tables(S, half, period, start, scale):
    inv = jnp.power(ROPE_THETA, -jnp.arange(half, dtype=F32) / half)
    ang = jnp.arange(S, dtype=F32)[:, None] * inv[None, :]
    cos, sin = jnp.cos(ang), jnp.sin(ang)
    e = np.arange(LANES) % period - start
    first = (e >= 0) & (e < half)
    second = (e >= half) & (e < 2 * half)
    jidx = np.where(first | second, e % half, 0)
    c = _lane_table(cos, jidx, first | second, 1.0)
    s1 = _lane_table(-sin, jidx, first, 0.0)
    s2 = _lane_table(sin, jidx, second, 0.0)
    return jnp.stack([c, s1, s2]) * scale


def _all_tables(S):
    mla_scale = float((MLA_NOPE + MLA_ROPE) ** -0.5 * np.log2(np.e))
    dil_scale = float(HEAD_DIM ** -0.5 * np.log2(np.e))
    return jnp.concatenate([
        _rope_tables(S, MLA_ROPE // 2, MLA_PAD, MLA_NOPE, mla_scale),
        _rope_tables(S, MLA_ROPE // 2, MLA_PAD, MLA_NOPE, 1.0),
        _rope_tables(S, DIL_ROT // 2, HEAD_DIM, 0, dil_scale),
        _rope_tables(S, DIL_ROT // 2, HEAD_DIM, 0, 1.0),
    ])


def _prep_weights(norm_mix, w_in_a, q_norm, w_uq, kv_norm, w_ukv, w_out_a, w_in_r, conv_w, conv_b,
                  lru_w_gate, lru_b_gate, lru_lambda, w_out_r, norm_ffn, w_gu, w_down, norm_final):
    w_in = w_in_a[0]
    qk_dim = MLA_NOPE + MLA_ROPE
    kr_cols = jnp.pad(w_in[:, MLA_Q_RANK + MLA_KV_RANK:MLA_IN],
                      ((0, 0), (MLA_NOPE, MLA_PAD - qk_dim)))
    w_mla = jnp.concatenate([w_in[:, :MLA_Q_RANK + MLA_KV_RANK], kr_cols], axis=1)
    uq = w_uq[0].reshape(MLA_Q_RANK, MLA_HEADS, qk_dim)
    uq = jnp.pad(uq, ((0, 0), (0, 0), (0, MLA_PAD - qk_dim))).reshape(MLA_Q_RANK, MLA_HEADS * MLA_PAD)
    ukv = w_ukv[0].reshape(MLA_KV_RANK, MLA_HEADS, MLA_NOPE + MLA_V)
    uk = jnp.pad(ukv[:, :, :MLA_NOPE], ((0, 0), (0, 0), (0, MLA_PAD - MLA_NOPE)))
    uk = uk.reshape(MLA_KV_RANK, MLA_HEADS * MLA_PAD)
    uv = ukv[:, :, MLA_NOPE:].reshape(MLA_KV_RANK, MLA_HEADS * MLA_V).T
    wg = lru_w_gate[0]
    w_gate = jnp.concatenate([wg[:, 0], wg[:, 1]], axis=-1)
    row = lambda v: v.reshape(1, -1)
    return {
        "norm_mix": [row(norm_mix[l]) for l in range(2)],
        "w_mla": w_mla.astype(BF16),
        "w_dil": w_in[:, MLA_IN:].astype(BF16),
        "q_norm": row(q_norm[0]),
        "w_uq": uq.astype(BF16),
        "kv_norm": row(kv_norm[0]),
        "w_uk": uk.astype(BF16),
        "w_uv": uv.astype(BF16),
        "w_out_a": w_out_a[0].astype(BF16),
        "w_in_r": w_in_r[0].astype(BF16),
        "conv_w": conv_w[0],
        "conv_b": row(conv_b[0]),
        "w_gate": w_gate.astype(BF16),
        "b_r": [row(lru_b_gate[0, d, 0]) for d in range(2)],
        "b_i": [row(lru_b_gate[0, d, 1]) for d in range(2)],
        "lam": [row(lru_lambda[0, d]) for d in range(2)],
        "w_out_r": w_out_r[0].astype(BF16),
        "norm_ffn": [row(norm_ffn[l]) for l in range(2)],
        "w_gu": [w_gu[l].astype(BF16) for l in range(2)],
        "w_down": [w_down[l].astype(BF16) for l in range(2)],
        "norm_final": row(norm_final),
    }


def _trunk(x, wp):
    B, S, D = x.shape
    T = B * S
    tabs = _all_tables(S)
    q, k, vt, *dgs = _l0_prep(x, wp["norm_mix"][0], wp, tabs, tm=512)
    tq, cq = min(S, 8192), 1024
    o_mla = _mla_attention(q, k, vt, tq=tq, tk=min(S, 2048), cq=cq, hps=min(4, max(2, 16 * cq // tq)))
    os_, ls_ = [], []
    for dg, (_, dil) in zip(dgs, DIL_PATTERNS):
        o_g, l_g = _dil_attention(dg, dil, tt=min(S // dil, 512))
        os_.append(o_g.reshape(T // dil, dil * DIL_W))
        ls_.append(l_g.reshape(T // dil, dil * DIL_W))
    x1 = _out_ffn_attn(x.reshape(T, D), o_mla.reshape(T, MLA_HEADS * MLA_V), os_, ls_, wp, 0, tm=512)
    y, xr = _l1_prep(x1, wp["norm_mix"][1], wp["w_in_r"], tm=512)
    xr = xr.reshape(B, S, D_RNN)
    hf, xconv = _rglru(xr, wp, ts=512, cb=D_RNN, reverse=False)
    a = _rglru(xconv, wp, ts=512, cb=D_RNN, reverse=True, hf=hf, y=y.reshape(B, S, D_RNN))
    out = _out_ffn_rec(x1, a.reshape(T, D_RNN), wp, 1, tm=512)
    return out.reshape(B, S, D)


def kernel(x_prompt, x_sample, norm_mix, w_in_a, q_norm, w_uq, kv_norm, w_ukv, w_out_a, w_in_r, conv_w,
           conv_b, lru_w_gate, lru_b_gate, lru_lambda, w_out_r, norm_ffn, w_gu, w_down, norm_final):
    wp = _prep_weights(norm_mix, w_in_a, q_norm, w_uq, kv_norm, w_ukv, w_out_a, w_in_r, conv_w, conv_b,
                       lru_w_gate, lru_b_gate, lru_lambda, w_out_r, norm_ffn, w_gu, w_down, norm_final)
    return (_trunk(x_prompt, wp), _trunk(x_sample, wp))
```

```python
import functools

import numpy as np
import jax
import jax.numpy as jnp
from jax import lax
from jax.experimental import pallas as pl
from jax.experimental.pallas import tpu as pltpu

F32 = jnp.float32
BF16 = jnp.bfloat16

D_MODEL = 1024
HEAD_DIM = 64
ROPE_THETA = 500000.0
NORM_EPS = 1e-6
MLA_HEADS = 8
MLA_NOPE = 64
MLA_ROPE = 32
MLA_V = 64
MLA_Q_RANK = 256
MLA_KV_RANK = 128
DIL_PATTERNS = ((128, 1), (512, 4), (2048, 16))
DIL_GROUPS = len(DIL_PATTERNS)
DIL_HEADS = 8
DIL_ROT = HEAD_DIM // 4
DIL_STEPS = 64
DIL_QB = 128
DIL_ROWS_PER_STEP = 512
MLA_IN = MLA_Q_RANK + MLA_KV_RANK + MLA_ROPE
DIL_W = DIL_HEADS * HEAD_DIM
DIL_QKV = 3 * DIL_GROUPS * DIL_W
D_RNN = 1536
LRU_BLOCKS = 12
LRU_BW = D_RNN // LRU_BLOCKS
CONV_W = 4
CONV_LEFT = 2
LRU_C = 8.0
D_FF = ((8 * D_MODEL // 3 + 255) // 256) * 256
NEG_BIG = -1e30

LANES = 128
SUBLANES = 8
MLA_PAD = 128
VMEM_LIMIT = 56 * 1024 * 1024

FF_CHUNK = 256
FF_CHUNKS = D_FF // FF_CHUNK


def _cparams(sem):
    return pltpu.CompilerParams(dimension_semantics=sem, vmem_limit_bytes=VMEM_LIMIT)


def _const_spec(shape):
    nd = len(shape)
    return pl.BlockSpec(shape, lambda *_: (0,) * nd, pipeline_mode=pl.Buffered(1))


def _rms(x, g):
    return x * lax.rsqrt(jnp.mean(x * x, axis=-1, keepdims=True) + NORM_EPS) * g


def _rope_block(xb, c, s1, s2, half):
    return xb * c + pltpu.roll(xb, LANES - half, 1) * s1 + pltpu.roll(xb, half, 1) * s2


def _l0_prep_kernel(x_ref, g_ref, wmla_ref, wdil_ref, qn_ref, wuq_ref, kvn_ref, wuk_ref, wuv_ref,
                    tab_ref, q_out, k_out, v_out, d0_out, d1_out, d2_out, perm_scr):
    tm = x_ref.shape[1]
    d_outs = (d0_out, d1_out, d2_out)
    x = x_ref[0]
    h = _rms(x, g_ref[...]).astype(BF16)
    z = jnp.dot(h, wmla_ref[...], preferred_element_type=F32)
    qn = _rms(z[:, :MLA_Q_RANK], qn_ref[...]).astype(BF16)
    kvn = _rms(z[:, MLA_Q_RANK:MLA_Q_RANK + MLA_KV_RANK], kvn_ref[...]).astype(BF16)
    kr = z[:, MLA_Q_RANK + MLA_KV_RANK:]
    qf = jnp.dot(qn, wuq_ref[...], preferred_element_type=F32)
    kf = jnp.dot(kvn, wuk_ref[...], preferred_element_type=F32)
    vt = lax.dot_general(wuv_ref[...], kvn, (((1,), (1,)), ((), ())),
                         preferred_element_type=F32)
    half = MLA_ROPE // 2
    krr = _rope_block(kr, tab_ref[3], tab_ref[4], tab_ref[5], half)
    for hh in range(MLA_HEADS):
        sl = slice(hh * MLA_PAD, (hh + 1) * MLA_PAD)
        q_out[0, :, sl] = _rope_block(qf[:, sl], tab_ref[0], tab_ref[1], tab_ref[2], half).astype(BF16)
        k_out[0, :, sl] = (kf[:, sl] + krr).astype(BF16)
    v_out[0] = vt.astype(BF16)
    dhalf = DIL_ROT // 2
    for j in range(3 * DIL_GROUPS):
        c, g = divmod(j, DIL_GROUPS)
        dil = DIL_PATTERNS[g][1]
        zc = jnp.dot(h, wdil_ref[:, j * DIL_W:(j + 1) * DIL_W], preferred_element_type=F32)
        for b in range(DIL_W // LANES):
            zb = zc[:, b * LANES:(b + 1) * LANES]
            if c < 2:
                t0 = 6 + 3 * c
                zb = _rope_block(zb, tab_ref[t0], tab_ref[t0 + 1], tab_ref[t0 + 2], dhalf)
            if dil == 1:
                d_outs[g][0, :, c * DIL_W + b * LANES:c * DIL_W + (b + 1) * LANES] = zb.astype(BF16)
            else:
                perm_scr[b] = zb
        if dil > 1:
            for r in range(dil):
                for b in range(DIL_W // LANES):
                    col = (r * 3 + c) * DIL_W + b * LANES
                    d_outs[g][0, :, col:col + LANES] = perm_scr[
                        b, pl.ds(r, tm // dil, stride=dil), :].astype(BF16)


def _l0_prep(x, g, wp, tabs, tm):
    B, S, D = x.shape
    grid = (B, S // tm)
    row = lambda w: pl.BlockSpec((1, tm, w), lambda b, i: (b, i, 0))
    return pl.pallas_call(
        _l0_prep_kernel,
        grid=grid,
        in_specs=[
            row(D),
            _const_spec((1, D)),
            _const_spec(wp["w_mla"].shape),
            _const_spec(wp["w_dil"].shape),
            _const_spec((1, MLA_Q_RANK)),
            _const_spec(wp["w_uq"].shape),
            _const_spec((1, MLA_KV_RANK)),
            _const_spec(wp["w_uk"].shape),
            _const_spec(wp["w_uv"].shape),
            pl.BlockSpec((12, tm, LANES), lambda b, i: (0, i, 0)),
        ],
        out_specs=[row(MLA_HEADS * MLA_PAD), row(MLA_HEADS * MLA_PAD),
                   pl.BlockSpec((1, MLA_HEADS * MLA_V, tm), lambda b, i: (b, 0, i))] + [
            pl.BlockSpec((1, tm // d, d * 3 * DIL_W), lambda b, i: (b, i, 0)) for _, d in DIL_PATTERNS],
        out_shape=[
            jax.ShapeDtypeStruct((B, S, MLA_HEADS * MLA_PAD), BF16),
            jax.ShapeDtypeStruct((B, S, MLA_HEADS * MLA_PAD), BF16),
            jax.ShapeDtypeStruct((B, MLA_HEADS * MLA_V, S), BF16),
        ] + [jax.ShapeDtypeStruct((B, S // d, d * 3 * DIL_W), BF16) for _, d in DIL_PATTERNS],
        scratch_shapes=[pltpu.VMEM((DIL_W // LANES, tm, LANES), F32)],
        compiler_params=_cparams(("parallel", "parallel")),
    )(x, g, wp["w_mla"], wp["w_dil"], wp["q_norm"], wp["w_uq"], wp["kv_norm"], wp["w_uk"], wp["w_uv"], tabs)


def _mla_kernel(q_ref, k_ref, vt_ref, o_ref, m_scr, acc_scr, *, nk, cq):
    ki = pl.program_id(3)

    @pl.when(ki == 0)
    def _():
        m_scr[...] = jnp.full(m_scr.shape, -jnp.inf, F32)
        acc_scr[...] = jnp.zeros(acc_scr.shape, F32)

    tq = q_ref.shape[1]
    hps = m_scr.shape[0]
    pair_shape = (2 * MLA_V, vt_ref.shape[2])
    own_rows = lax.broadcasted_iota(jnp.int32, pair_shape, 0) < MLA_V
    ones = jnp.ones(pair_shape, BF16)
    v1 = []
    for pair in range(hps // 2):
        vt = vt_ref[0, pair * 2 * MLA_V:(pair + 1) * 2 * MLA_V, :]
        v1 += [jnp.where(own_rows, vt, ones), jnp.where(own_rows, ones, vt)]

    def scores(hh, c):
        q = q_ref[0, c * cq:(c + 1) * cq, hh * MLA_PAD:(hh + 1) * MLA_PAD]
        k = k_ref[0, :, hh * MLA_PAD:(hh + 1) * MLA_PAD]
        st = lax.dot_general(k, q, (((1,), (1,)), ((), ())), preferred_element_type=F32)
        m_prev = m_scr[hh, :, c * cq:(c + 1) * cq]
        m_new = jnp.maximum(m_prev, jnp.max(st, axis=0, keepdims=True))
        m_scr[hh, :, c * cq:(c + 1) * cq] = m_new
        return st, m_prev, m_new

    def accumulate(hh, c, st, m_prev, m_new):
        alpha = jnp.exp2(m_prev - m_new)
        p = jnp.exp2(st - m_new).astype(BF16)
        acc_scr[hh, :, c * cq:(c + 1) * cq] = alpha * acc_scr[hh, :, c * cq:(c + 1) * cq] + jnp.dot(
            v1[hh], p, preferred_element_type=F32)

    tasks = [(hh, c) for hh in range(hps) for c in range(tq // cq)]
    pending = scores(*tasks[0])
    for t, task in enumerate(tasks):
        nxt = scores(*tasks[t + 1]) if t + 1 < len(tasks) else None
        accumulate(*task, *pending)
        pending = nxt

    @pl.when(ki == nk - 1)
    def _():
        for pair in range(hps // 2):
            a0, a1 = acc_scr[2 * pair], acc_scr[2 * pair + 1]
            ot = jnp.concatenate([a0[:MLA_V] / a0[MLA_V:MLA_V + 1], a1[MLA_V:] / a1[0:1]], axis=0)
            o_ref[0, :, pair * 2 * MLA_V:(pair + 1) * 2 * MLA_V] = ot.T.astype(BF16)


def _mla_attention(q, k, vt, tq, tk, cq=1024, hps=2):
    B, S, _ = q.shape
    nq, nk = S // tq, S // tk
    return pl.pallas_call(
        functools.partial(_mla_kernel, nk=nk, cq=cq),
        grid=(B, MLA_HEADS // hps, nq, nk),
        in_specs=[
            pl.BlockSpec((1, tq, hps * MLA_PAD), lambda b, h, i, j: (b, i, h)),
            pl.BlockSpec((1, tk, hps * MLA_PAD), lambda b, h, i, j: (b, j, h)),
            pl.BlockSpec((1, hps * MLA_V, tk), lambda b, h, i, j: (b, h, j)),
        ],
        out_specs=pl.BlockSpec((1, tq, hps * MLA_V), lambda b, h, i, j: (b, i, h)),
        out_shape=jax.ShapeDtypeStruct((B, S, MLA_HEADS * MLA_V), BF16),
        scratch_shapes=[
            pltpu.VMEM((hps, 1, tq), F32),
            pltpu.VMEM((hps, 2 * MLA_V, tq), F32),
        ],
        compiler_params=_cparams(("parallel", "parallel", "parallel", "arbitrary")),
    )(q, k, vt)


def _dil_kernel(prev_ref, cur_ref, next_ref, o_ref, lse_ref, *, nt, tt, nres):
    i = pl.program_id(2)
    nsub = tt // DIL_QB
    nkx = DIL_QB + 2 * DIL_STEPS
    qi = lax.broadcasted_iota(jnp.int32, (DIL_QB, nkx), 0)
    kx = lax.broadcasted_iota(jnp.int32, (DIL_QB, nkx), 1)
    rel = kx - DIL_STEPS - qi
    band = (rel >= -DIL_STEPS) & (rel <= DIL_STEPS)
    lane_q = lax.broadcasted_iota(jnp.int32, (DIL_QB, LANES), 1)
    first = lane_q < HEAD_DIM
    zero = jnp.zeros((DIL_QB, LANES), BF16)
    for res in range(nres):
        qc, kc, vc = (res * 3 * DIL_W + c * DIL_W for c in range(3))
        kext = jnp.concatenate([r[0, :, kc:kc + DIL_W] for r in (prev_ref, cur_ref, next_ref)], axis=0)
        vext = jnp.concatenate([r[0, :, vc:vc + DIL_W] for r in (prev_ref, cur_ref, next_ref)], axis=0)
        for j in range(nsub):
            ok = band
            if j == 0:
                ok = ok & ((kx >= DIL_STEPS) | (i > 0))
            if j == nsub - 1:
                ok = ok & ((kx < DIL_QB + DIL_STEPS) | (i < nt - 1))
            ok2 = jnp.concatenate([ok, ok], axis=0)
            rows = slice(j * DIL_QB, (j + 1) * DIL_QB)
            krows = slice(j * DIL_QB, j * DIL_QB + nkx)
            for hp in range(DIL_HEADS // 2):
                sl = slice(hp * LANES, (hp + 1) * LANES)
                qp = cur_ref[0, rows, qc + hp * LANES:qc + (hp + 1) * LANES]
                kp = kext[krows, sl]
                vp = vext[krows, sl]
                q2 = jnp.concatenate([jnp.where(first, qp, zero), jnp.where(first, zero, qp)], axis=0)
                s = lax.dot_general(q2, kp, (((1,), (1,)), ((), ())), preferred_element_type=F32)
                s = jnp.where(ok2, s, NEG_BIG)
                m = jnp.max(s, axis=1, keepdims=True)
                e = jnp.exp2(s - m)
                den = jnp.sum(e, axis=1, keepdims=True)
                o = jnp.dot(e.astype(BF16), vp, preferred_element_type=F32) / den
                lse = jnp.broadcast_to(m + jnp.log2(den), (2 * DIL_QB, LANES))
                oc = res * DIL_W + hp * LANES
                o_ref[0, rows, oc:oc + LANES] = jnp.where(first, o[:DIL_QB], o[DIL_QB:])
                lse_ref[0, rows, oc:oc + LANES] = jnp.where(first, lse[:DIL_QB], lse[DIL_QB:])


def _dil_attention(x, dil, tt):
    B, L, _ = x.shape
    nt = L // tt
    nres = max(1, min(dil, DIL_ROWS_PER_STEP // tt))
    hb = tt // DIL_STEPS
    nhb = L // DIL_STEPS
    w_in, w_out = nres * 3 * DIL_W, nres * DIL_W
    cur = pl.BlockSpec((1, tt, w_in), lambda b, r, i: (b, i, r))
    prev = pl.BlockSpec((1, DIL_STEPS, w_in), lambda b, r, i: (b, jnp.maximum(i * hb - 1, 0), r))
    nxt = pl.BlockSpec((1, DIL_STEPS, w_in), lambda b, r, i: (b, jnp.minimum((i + 1) * hb, nhb - 1), r))
    out_spec = pl.BlockSpec((1, tt, w_out), lambda b, r, i: (b, i, r))
    return pl.pallas_call(
        functools.partial(_dil_kernel, nt=nt, tt=tt, nres=nres),
        grid=(B, dil // nres, nt),
        in_specs=[prev, cur, nxt],
        out_specs=[out_spec, out_spec],
        out_shape=[jax.ShapeDtypeStruct((B, L, dil * DIL_W), F32)] * 2,
        compiler_params=_cparams(("parallel", "parallel", "parallel")),
    )(x, x, x)


def _ffn(x1, gf, wgu_ref, wdown_ref):
    h2 = _rms(x1, gf).astype(BF16)
    acc = x1
    for c in range(FF_CHUNKS):
        gate = jnp.dot(h2, wgu_ref[:, c * FF_CHUNK:(c + 1) * FF_CHUNK], preferred_element_type=F32)
        up = jnp.dot(h2, wgu_ref[:, D_FF + c * FF_CHUNK:D_FF + (c + 1) * FF_CHUNK],
                     preferred_element_type=F32)
        act = (gate * jax.nn.sigmoid(gate) * up).astype(BF16)
        acc = acc + jnp.dot(act, wdown_ref[c * FF_CHUNK:(c + 1) * FF_CHUNK, :], preferred_element_type=F32)
    return acc


def _out_ffn_attn_kernel(x_ref, om_ref, o0_ref, o1_ref, o2_ref, l0_ref, l1_ref, l2_ref,
                         wo_ref, gf_ref, wgu_ref, wdown_ref, y_ref, perm_scr):
    tm = x_ref.shape[0]

    def natural(ref, slot, dil):
        for r in range(dil):
            for b in range(DIL_W // LANES):
                perm_scr[slot, b, pl.ds(r, tm // dil, stride=dil), :] = ref[
                    :, r * DIL_W + b * LANES:r * DIL_W + (b + 1) * LANES]
        return jnp.concatenate([perm_scr[slot, b] for b in range(DIL_W // LANES)], axis=1)

    d1, d2 = DIL_PATTERNS[1][1], DIL_PATTERNS[2][1]
    l0, l1, l2 = l0_ref[...], natural(l1_ref, 0, d1), natural(l2_ref, 1, d2)
    mx = jnp.maximum(jnp.maximum(l0, l1), l2)
    e0, e1, e2 = jnp.exp2(l0 - mx), jnp.exp2(l1 - mx), jnp.exp2(l2 - mx)
    od = (e0 * o0_ref[...] + e1 * natural(o1_ref, 2, d1) + e2 * natural(o2_ref, 3, d2)) / (e0 + e1 + e2)
    o = jnp.concatenate([om_ref[...], od.astype(BF16)], axis=1)
    x1 = x_ref[...] + jnp.dot(o, wo_ref[...], preferred_element_type=F32)
    y_ref[...] = _ffn(x1, gf_ref[...], wgu_ref, wdown_ref)


def _out_ffn_rec_kernel(x_ref, a_ref, wo_ref, gf_ref, wgu_ref, wdown_ref, gfin_ref, y_ref):
    x1 = x_ref[...] + jnp.dot(a_ref[...], wo_ref[...], preferred_element_type=F32)
    y_ref[...] = _rms(_ffn(x1, gf_ref[...], wgu_ref, wdown_ref), gfin_ref[...])


def _out_ffn_attn(x, om, os_, ls_, wp, layer, tm):
    T, D = x.shape
    row = lambda w: pl.BlockSpec((tm, w), lambda i: (i, 0))
    grp = [pl.BlockSpec((tm // d, d * DIL_W), lambda i: (i, 0)) for _, d in DIL_PATTERNS]
    return pl.pallas_call(
        _out_ffn_attn_kernel,
        grid=(T // tm,),
        in_specs=[row(D), row(DIL_W)] + grp + grp + [
            _const_spec((MLA_HEADS * MLA_V + DIL_W, D)), _const_spec((1, D)),
            _const_spec((D, 2 * D_FF)), _const_spec((D_FF, D))],
        out_specs=row(D),
        out_shape=jax.ShapeDtypeStruct((T, D), F32),
        scratch_shapes=[pltpu.VMEM((4, DIL_W // LANES, tm, LANES), F32)],
        compiler_params=_cparams(("parallel",)),
    )(x, om, *os_, *ls_, wp["w_out_a"], wp["norm_ffn"][layer],
      wp["w_gu"][layer], wp["w_down"][layer])


def _out_ffn_rec(x, a, wp, layer, tm):
    T, D = x.shape
    row = lambda w: pl.BlockSpec((tm, w), lambda i: (i, 0))
    return pl.pallas_call(
        _out_ffn_rec_kernel,
        grid=(T // tm,),
        in_specs=[row(D), row(D_RNN), _const_spec((D_RNN, D)), _const_spec((1, D)),
                  _const_spec((D, 2 * D_FF)), _const_spec((D_FF, D)), _const_spec((1, D))],
        out_specs=row(D),
        out_shape=jax.ShapeDtypeStruct((T, D), F32),
        compiler_params=_cparams(("parallel",)),
    )(x, a, wp["w_out_r"], wp["norm_ffn"][layer], wp["w_gu"][layer], wp["w_down"][layer],
      wp["norm_final"])


def _l1_prep_kernel(x_ref, g_ref, w_ref, y_ref, xr_ref):
    h = _rms(x_ref[...], g_ref[...]).astype(BF16)
    zy = jnp.dot(h, w_ref[:, :D_RNN], preferred_element_type=F32)
    y_ref[...] = (0.5 * zy * (1.0 + jnp.tanh(np.sqrt(2.0 / np.pi).astype(np.float32)
                                             * (zy + 0.044715 * (zy * zy * zy))))).astype(BF16)
    xr_ref[...] = jnp.dot(h, w_ref[:, D_RNN:], preferred_element_type=F32)


def _l1_prep(x, g, w, tm):
    T, D = x.shape
    row = lambda w_: pl.BlockSpec((tm, w_), lambda i: (i, 0))
    return pl.pallas_call(
        _l1_prep_kernel,
        grid=(T // tm,),
        in_specs=[row(D), _const_spec((1, D)), _const_spec((D, 2 * D_RNN))],
        out_specs=[row(D_RNN), row(D_RNN)],
        out_shape=[jax.ShapeDtypeStruct((T, D_RNN), BF16), jax.ShapeDtypeStruct((T, D_RNN), F32)],
        compiler_params=_cparams(("parallel",)),
    )(x, g, w)


def _rglru_kernel(*refs, ns, ts, cb, reverse):
    if reverse:
        (xconv_ref, wg_ref, br_ref, bi_ref, lam_ref, hf_ref, y_ref,
         o_ref, a_scr, u_scr, h_scr, carry_scr) = refs
    else:
        (xp_ref, xc_ref, xn_ref, cw_ref, cbias_ref, wg_ref, br_ref, bi_ref, lam_ref,
         o_ref, xconv_ref, a_scr, u_scr, carry_scr) = refs
        h_scr = o_ref.at[0]
    step = pl.program_id(2)

    @pl.when(step == 0)
    def _():
        carry_scr[...] = jnp.zeros(carry_scr.shape, F32)

    if reverse:
        xc = xconv_ref[0]
    else:
        x0 = xc_ref[0]
        rows = lax.broadcasted_iota(jnp.int32, (ts, cb), 0)
        pv = jnp.where(step > 0, xp_ref[0], 0.0)
        nv = jnp.where(step < ns - 1, xn_ref[0], 0.0)
        xm1 = jnp.where(rows == 0, pv[SUBLANES - 1:SUBLANES], pltpu.roll(x0, 1, 0))
        xm2 = pltpu.roll(x0, 2, 0)
        xm2 = jnp.where(rows == 0, pv[SUBLANES - 2:SUBLANES - 1], xm2)
        xm2 = jnp.where(rows == 1, pv[SUBLANES - 1:SUBLANES], xm2)
        xp1 = jnp.where(rows == ts - 1, nv[0:1], pltpu.roll(x0, ts - 1, 0))
        xc = (xm2 * cw_ref[0:1] + xm1 * cw_ref[1:2] + x0 * cw_ref[2:3] + xp1 * cw_ref[3:4]) + cbias_ref[...]
        xconv_ref[0] = xc

    xcb = xc.astype(BF16)
    for j in range(cb // LRU_BW):
        sl = slice(j * LRU_BW, (j + 1) * LRU_BW)
        gts = jnp.dot(xcb[:, sl], wg_ref[j], preferred_element_type=F32)
        r = jax.nn.sigmoid(gts[:, :LRU_BW] + br_ref[:, sl])
        ig = jax.nn.sigmoid(gts[:, LRU_BW:] + bi_ref[:, sl])
        nlam = -lam_ref[:, sl]
        softplus = jnp.maximum(nlam, 0.0) + jnp.log1p(jnp.exp(-jnp.abs(nlam)))
        a = jnp.exp((-LRU_C) * r * softplus)
        a_scr[:, sl] = a
        u_scr[:, sl] = jnp.sqrt(jnp.maximum(1.0 - a * a, 0.0)) * (ig * xc[:, sl])

    row8 = lax.broadcasted_iota(jnp.int32, (SUBLANES, cb), 0)
    nchunk = ts // SUBLANES

    def chunk(c, hprev):
        idx = (nchunk - 1 - c) if reverse else c
        r0 = pl.multiple_of(idx * SUBLANES, SUBLANES)
        a = a_scr[pl.ds(r0, SUBLANES), :]
        b = u_scr[pl.ds(r0, SUBLANES), :]
        for d in (1, 2, 4):
            if reverse:
                keep = row8 < SUBLANES - d
                sh = SUBLANES - d
            else:
                keep = row8 >= d
                sh = d
            a_s = jnp.where(keep, pltpu.roll(a, sh, 0), 1.0)
            b_s = jnp.where(keep, pltpu.roll(b, sh, 0), 0.0)
            b = a * b_s + b
            a = a * a_s
        h = a * hprev + b
        h_scr[pl.ds(r0, SUBLANES), :] = h
        edge = h[0:1] if reverse else h[SUBLANES - 1:SUBLANES]
        return jnp.broadcast_to(edge, (SUBLANES, cb))

    carry_scr[...] = lax.fori_loop(0, nchunk, chunk, carry_scr[...], unroll=4)

    if reverse:
        o_ref[0] = ((hf_ref[0] + h_scr[...]) * y_ref[0].astype(F32)).astype(BF16)


def _rglru(xr, wp, ts, cb, reverse, hf=None, y=None):
    B, S, C = xr.shape
    ns = S // ts
    nhb = S // SUBLANES
    hb = ts // SUBLANES
    d = 1 if reverse else 0
    tidx = (lambda i: ns - 1 - i) if reverse else (lambda i: i)
    cur = pl.BlockSpec((1, ts, cb), lambda b, c, i: (b, tidx(i), c))
    vec = pl.BlockSpec((1, cb), lambda b, c, i: (0, c))
    gate_specs = [pl.BlockSpec((cb // LRU_BW, LRU_BW, 2 * LRU_BW), lambda b, c, i: (c, 0, 0)), vec, vec, vec]
    gate_args = [wp["w_gate"][d], wp["b_r"][d], wp["b_i"][d], wp["lam"][d]]
    tile = pltpu.VMEM((ts, cb), F32)
    carry = pltpu.VMEM((SUBLANES, cb), F32)
    if reverse:
        in_specs = [cur] + gate_specs + [cur, cur]
        args = [xr] + gate_args + [hf, y]
        scratch = [tile, tile, tile, carry]
        out_specs, out_shape = cur, jax.ShapeDtypeStruct((B, S, C), BF16)
    else:
        prev = pl.BlockSpec((1, SUBLANES, cb), lambda b, c, i: (b, jnp.maximum(i * hb - 1, 0), c))
        nxt = pl.BlockSpec((1, SUBLANES, cb), lambda b, c, i: (b, jnp.minimum((i + 1) * hb, nhb - 1), c))
        in_specs = [prev, cur, nxt, pl.BlockSpec((CONV_W, cb), lambda b, c, i: (0, c)), vec] + gate_specs
        args = [xr, xr, xr, wp["conv_w"], wp["conv_b"]] + gate_args
        scratch = [tile, tile, carry]
        out_specs, out_shape = [cur, cur], [jax.ShapeDtypeStruct((B, S, C), F32)] * 2
    return pl.pallas_call(
        functools.partial(_rglru_kernel, ns=ns, ts=ts, cb=cb, reverse=reverse),
        grid=(B, C // cb, ns),
        in_specs=in_specs,
        out_specs=out_specs,
        out_shape=out_shape,
        scratch_shapes=scratch,
        compiler_params=_cparams(("parallel", "parallel", "arbitrary")),
    )(*args)


def _lane_table(vals, jidx, mask, fill):
    return jnp.where(mask[None, :], vals[:, jidx], fill)


def _rope_tables(S, half, period, start, scale):
    inv = jnp.power(ROPE_THETA, -jnp.arange(half, dtype=F32) / half)
    ang = jnp.arange(S, dtype=F32)[:, None] * inv[None, :]
    cos, sin = jnp.cos(ang), jnp.sin(ang)
    e = np.arange(LANES) % period - start
    first = (e >= 0) & (e < half)
    second = (e >= half) & (e < 2 * half)
    jidx = np.where(first | second, e % half, 0)
    c = _lane_table(cos, jidx, first | second, 1.0)
    s1 = _lane_table(-sin, jidx, first, 0.0)
    s2 = _lane_table(sin, jidx, second, 0.0)
    return jnp.stack([c, s1, s2]) * scale


def _all_tables(S):
    mla_scale = float((MLA_NOPE + MLA_ROPE) ** -0.5 * np.log2(np.e))
    dil_scale = float(HEAD_DIM ** -0.5 * np.log2(np.e))
    return jnp.concatenate([
        _rope_tables(S, MLA_ROPE // 2, MLA_PAD, MLA_NOPE, mla_scale),
        _rope_tables(S, MLA_ROPE // 2, MLA_PAD, MLA_NOPE, 1.0),
        _rope_tables(S, DIL_ROT // 2, HEAD_DIM, 0, dil_scale),
        _rope_tables(S, DIL_ROT // 2, HEAD_DIM, 0, 1.0),
    ])


def _prep_weights(norm_mix, w_in_a, q_norm, w_uq, kv_norm, w_ukv, w_out_a, w_in_r, conv_w, conv_b,
                  lru_w_gate, lru_b_gate, lru_lambda, w_out_r, norm_ffn, w_gu, w_down, norm_final):
    w_in = w_in_a[0]
    qk_dim = MLA_NOPE + MLA_ROPE
    kr_cols = jnp.pad(w_in[:, MLA_Q_RANK + MLA_KV_RANK:MLA_IN],
                      ((0, 0), (MLA_NOPE, MLA_PAD - qk_dim)))
    w_mla = jnp.concatenate([w_in[:, :MLA_Q_RANK + MLA_KV_RANK], kr_cols], axis=1)
    uq = w_uq[0].reshape(MLA_Q_RANK, MLA_HEADS, qk_dim)
    uq = jnp.pad(uq, ((0, 0), (0, 0), (0, MLA_PAD - qk_dim))).reshape(MLA_Q_RANK, MLA_HEADS * MLA_PAD)
    ukv = w_ukv[0].reshape(MLA_KV_RANK, MLA_HEADS, MLA_NOPE + MLA_V)
    uk = jnp.pad(ukv[:, :, :MLA_NOPE], ((0, 0), (0, 0), (0, MLA_PAD - MLA_NOPE)))
    uk = uk.reshape(MLA_KV_RANK, MLA_HEADS * MLA_PAD)
    uv = ukv[:, :, MLA_NOPE:].reshape(MLA_KV_RANK, MLA_HEADS * MLA_V).T
    wg = lru_w_gate[0]
    w_gate = jnp.concatenate([wg[:, 0], wg[:, 1]], axis=-1)
    row = lambda v: v.reshape(1, -1)
    return {
        "norm_mix": [row(norm_mix[l]) for l in range(2)],
        "w_mla": w_mla.astype(BF16),
        "w_dil": w_in[:, MLA_IN:].astype(BF16),
        "q_norm": row(q_norm[0]),
        "w_uq": uq.astype(BF16),
        "kv_norm": row(kv_norm[0]),
        "w_uk": uk.astype(BF16),
        "w_uv": uv.astype(BF16),
        "w_out_a": w_out_a[0].astype(BF16),
        "w_in_r": w_in_r[0].astype(BF16),
        "conv_w": conv_w[0],
        "conv_b": row(conv_b[0]),
        "w_gate": w_gate.astype(BF16),
        "b_r": [row(lru_b_gate[0, d, 0]) for d in range(2)],
        "b_i": [row(lru_b_gate[0, d, 1]) for d in range(2)],
        "lam": [row(lru_lambda[0, d]) for d in range(2)],
        "w_out_r": w_out_r[0].astype(BF16),
        "norm_ffn": [row(norm_ffn[l]) for l in range(2)],
        "w_gu": [w_gu[l].astype(BF16) for l in range(2)],
        "w_down": [w_down[l].astype(BF16) for l in range(2)],
        "norm_final": row(norm_final),
    }


def _trunk(x, wp):
    B, S, D = x.shape
    T = B * S
    tabs = _all_tables(S)
    q, k, vt, *dgs = _l0_prep(x, wp["norm_mix"][0], wp, tabs, tm=512)
    tq, cq = min(S, 8192), 1024
    o_mla = _mla_attention(q, k, vt, tq=tq, tk=min(S, 2048), cq=cq, hps=min(4, max(2, 16 * cq // tq)))
    os_, ls_ = [], []
    for dg, (_, dil) in zip(dgs, DIL_PATTERNS):
        o_g, l_g = _dil_attention(dg, dil, tt=min(S // dil, 512))
        os_.append(o_g.reshape(T // dil, dil * DIL_W))
        ls_.append(l_g.reshape(T // dil, dil * DIL_W))
    x1 = _out_ffn_attn(x.reshape(T, D), o_mla.reshape(T, MLA_HEADS * MLA_V), os_, ls_, wp, 0, tm=512)
    y, xr = _l1_prep(x1, wp["norm_mix"][1], wp["w_in_r"], tm=512)
    xr = xr.reshape(B, S, D_RNN)
    hf, xconv = _rglru(xr, wp, ts=512, cb=D_RNN, reverse=False)
    a = _rglru(xconv, wp, ts=512, cb=D_RNN, reverse=True, hf=hf, y=y.reshape(B, S, D_RNN))
    out = _out_ffn_rec(x1, a.reshape(T, D_RNN), wp, 1, tm=512)
    return out.reshape(B, S, D)


def kernel(x_prompt, x_sample, norm_mix, w_in_a, q_norm, w_uq, kv_norm, w_ukv, w_out_a, w_in_r, conv_w,
           conv_b, lru_w_gate, lru_b_gate, lru_lambda, w_out_r, norm_ffn, w_gu, w_down, norm_final):
    wp = _prep_weights(norm_mix, w_in_a, q_norm, w_uq, kv_norm, w_ukv, w_out_a, w_in_r, conv_w, conv_b,
                       lru_w_gate, lru_b_gate, lru_lambda, w_out_r, norm_ffn, w_gu, w_down, norm_final)
    return (_trunk(x_prompt, wp), _trunk(x_sample, wp))
```

```python
import functools

import numpy as np
import jax
import jax.numpy as jnp
from jax import lax
from jax.experimental import pallas as pl
from jax.experimental.pallas import tpu as pltpu

F32 = jnp.float32
BF16 = jnp.bfloat16

D_MODEL = 1024
HEAD_DIM = 64
ROPE_THETA = 500000.0
NORM_EPS = 1e-6
MLA_HEADS = 8
MLA_NOPE = 64
MLA_ROPE = 32
MLA_V = 64
MLA_Q_RANK = 256
MLA_KV_RANK = 128
DIL_PATTERNS = ((128, 1), (512, 4), (2048, 16))
DIL_GROUPS = len(DIL_PATTERNS)
DIL_HEADS = 8
DIL_ROT = HEAD_DIM // 4
DIL_STEPS = 64
DIL_QB = 128
DIL_ROWS_PER_STEP = 512
MLA_IN = MLA_Q_RANK + MLA_KV_RANK + MLA_ROPE
DIL_W = DIL_HEADS * HEAD_DIM
DIL_QKV = 3 * DIL_GROUPS * DIL_W
D_RNN = 1536
LRU_BLOCKS = 12
LRU_BW = D_RNN // LRU_BLOCKS
CONV_W = 4
CONV_LEFT = 2
LRU_C = 8.0
D_FF = ((8 * D_MODEL // 3 + 255) // 256) * 256
NEG_BIG = -1e30
MLA_SCALE = float((MLA_NOPE + MLA_ROPE) ** -0.5 * np.log2(np.e))
DIL_SCALE = float(HEAD_DIM ** -0.5 * np.log2(np.e))

LANES = 128
SUBLANES = 8
MLA_PAD = 128
VMEM_LIMIT = 56 * 1024 * 1024

FF_CHUNK = 256
FF_CHUNKS = D_FF // FF_CHUNK


def _cparams(sem):
    return pltpu.CompilerParams(dimension_semantics=sem, vmem_limit_bytes=VMEM_LIMIT)


def _const_spec(shape):
    nd = len(shape)
    return pl.BlockSpec(shape, lambda *_: (0,) * nd, pipeline_mode=pl.Buffered(1))


def _rms(x, g):
    return x * lax.rsqrt(jnp.mean(x * x, axis=-1, keepdims=True) + NORM_EPS) * g


def _rope_block(xb, c, s1, s2, half):
    return xb * c + pltpu.roll(xb, LANES - half, 1) * s1 + pltpu.roll(xb, half, 1) * s2


def _l0_prep_kernel(x_ref, g_ref, wmla_ref, wdil_ref, qn_ref, wuq_ref, kvn_ref, wuk_ref, wuv_ref,
                    tab_ref, q_out, k_out, v_out, d0_out, d1_out, d2_out, perm_scr):
    tm = x_ref.shape[1]
    d_outs = (d0_out, d1_out, d2_out)
    x = x_ref[0]
    h = _rms(x, g_ref[...]).astype(BF16)
    z = jnp.dot(h, wmla_ref[...], preferred_element_type=F32)
    qn = _rms(z[:, :MLA_Q_RANK], qn_ref[...]).astype(BF16)
    kvn = _rms(z[:, MLA_Q_RANK:MLA_Q_RANK + MLA_KV_RANK], kvn_ref[...]).astype(BF16)
    kr = z[:, MLA_Q_RANK + MLA_KV_RANK:]
    qf = jnp.dot(qn, wuq_ref[...], preferred_element_type=F32)
    kf = jnp.dot(kvn, wuk_ref[...], preferred_element_type=F32)
    vt = lax.dot_general(wuv_ref[...], kvn, (((1,), (1,)), ((), ())),
                         preferred_element_type=F32)
    half = MLA_ROPE // 2
    krr = _rope_block(kr, tab_ref[0], tab_ref[1], tab_ref[2], half)
    for hh in range(MLA_HEADS):
        sl = slice(hh * MLA_PAD, (hh + 1) * MLA_PAD)
        q_out[0, :, sl] = _rope_block(qf[:, sl] * MLA_SCALE, tab_ref[0], tab_ref[1], tab_ref[2], half).astype(BF16)
        k_out[0, :, sl] = (kf[:, sl] + krr).astype(BF16)
    v_out[0] = vt.astype(BF16)
    dhalf = DIL_ROT // 2
    for j in range(3 * DIL_GROUPS):
        c, g = divmod(j, DIL_GROUPS)
        dil = DIL_PATTERNS[g][1]
        zc = jnp.dot(h, wdil_ref[:, j * DIL_W:(j + 1) * DIL_W], preferred_element_type=F32)
        for b in range(DIL_W // LANES):
            zb = zc[:, b * LANES:(b + 1) * LANES]
            if c == 0:
                zb = zb * DIL_SCALE
            if c < 2:
                zb = _rope_block(zb, tab_ref[3], tab_ref[4], tab_ref[5], dhalf)
            if dil == 1:
                d_outs[g][0, :, c * DIL_W + b * LANES:c * DIL_W + (b + 1) * LANES] = zb.astype(BF16)
            else:
                perm_scr[b] = zb
        if dil > 1:
            for r in range(dil):
                for b in range(DIL_W // LANES):
                    col = (r * 3 + c) * DIL_W + b * LANES
                    d_outs[g][0, :, col:col + LANES] = perm_scr[
                        b, pl.ds(r, tm // dil, stride=dil), :].astype(BF16)


def _l0_prep(x, g, wp, tabs, tm):
    B, S, D = x.shape
    grid = (B, S // tm)
    row = lambda w: pl.BlockSpec((1, tm, w), lambda b, i: (b, i, 0))
    return pl.pallas_call(
        _l0_prep_kernel,
        grid=grid,
        in_specs=[
            row(D),
            _const_spec((1, D)),
            _const_spec(wp["w_mla"].shape),
            _const_spec(wp["w_dil"].shape),
            _const_spec((1, MLA_Q_RANK)),
            _const_spec(wp["w_uq"].shape),
            _const_spec((1, MLA_KV_RANK)),
            _const_spec(wp["w_uk"].shape),
            _const_spec(wp["w_uv"].shape),
            pl.BlockSpec((6, tm, LANES), lambda b, i: (0, i, 0)),
        ],
        out_specs=[row(MLA_HEADS * MLA_PAD), row(MLA_HEADS * MLA_PAD),
                   pl.BlockSpec((1, MLA_HEADS * MLA_V, tm), lambda b, i: (b, 0, i))] + [
            pl.BlockSpec((1, tm // d, d * 3 * DIL_W), lambda b, i: (b, i, 0)) for _, d in DIL_PATTERNS],
        out_shape=[
            jax.ShapeDtypeStruct((B, S, MLA_HEADS * MLA_PAD), BF16),
            jax.ShapeDtypeStruct((B, S, MLA_HEADS * MLA_PAD), BF16),
            jax.ShapeDtypeStruct((B, MLA_HEADS * MLA_V, S), BF16),
        ] + [jax.ShapeDtypeStruct((B, S // d, d * 3 * DIL_W), BF16) for _, d in DIL_PATTERNS],
        scratch_shapes=[pltpu.VMEM((DIL_W // LANES, tm, LANES), F32)],
        compiler_params=_cparams(("parallel", "parallel")),
    )(x, g, wp["w_mla"], wp["w_dil"], wp["q_norm"], wp["w_uq"], wp["kv_norm"], wp["w_uk"], wp["w_uv"], tabs)


def _mla_kernel(q_ref, k_ref, vt_ref, o_ref, m_scr, acc_scr, *, nk, cq):
    ki = pl.program_id(3)

    @pl.when(ki == 0)
    def _():
        m_scr[...] = jnp.full(m_scr.shape, -jnp.inf, F32)
        acc_scr[...] = jnp.zeros(acc_scr.shape, F32)

    tq = q_ref.shape[1]
    hps = m_scr.shape[0]
    pair_shape = (2 * MLA_V, vt_ref.shape[2])
    own_rows = lax.broadcasted_iota(jnp.int32, pair_shape, 0) < MLA_V
    ones = jnp.ones(pair_shape, BF16)
    v1 = []
    for pair in range(hps // 2):
        vt = vt_ref[0, pair * 2 * MLA_V:(pair + 1) * 2 * MLA_V, :]
        v1 += [jnp.where(own_rows, vt, ones), jnp.where(own_rows, ones, vt)]

    def scores(hh, c):
        q = q_ref[0, c * cq:(c + 1) * cq, hh * MLA_PAD:(hh + 1) * MLA_PAD]
        k = k_ref[0, :, hh * MLA_PAD:(hh + 1) * MLA_PAD]
        st = lax.dot_general(k, q, (((1,), (1,)), ((), ())), preferred_element_type=F32)
        m_prev = m_scr[hh, :, c * cq:(c + 1) * cq]
        m_new = jnp.maximum(m_prev, jnp.max(st, axis=0, keepdims=True))
        m_scr[hh, :, c * cq:(c + 1) * cq] = m_new
        return st, m_prev, m_new

    def accumulate(hh, c, st, m_prev, m_new):
        alpha = jnp.exp2(m_prev - m_new)
        p = jnp.exp2(st - m_new).astype(BF16)
        acc_scr[hh, :, c * cq:(c + 1) * cq] = alpha * acc_scr[hh, :, c * cq:(c + 1) * cq] + jnp.dot(
            v1[hh], p, preferred_element_type=F32)

    tasks = [(hh, c) for hh in range(hps) for c in range(tq // cq)]
    pending = scores(*tasks[0])
    for t, task in enumerate(tasks):
        nxt = scores(*tasks[t + 1]) if t + 1 < len(tasks) else None
        accumulate(*task, *pending)
        pending = nxt

    @pl.when(ki == nk - 1)
    def _():
        for pair in range(hps // 2):
            a0, a1 = acc_scr[2 * pair], acc_scr[2 * pair + 1]
            ot = jnp.concatenate([a0[:MLA_V] / a0[MLA_V:MLA_V + 1], a1[MLA_V:] / a1[0:1]], axis=0)
            o_ref[0, :, pair * 2 * MLA_V:(pair + 1) * 2 * MLA_V] = ot.T.astype(BF16)


def _mla_attention(q, k, vt, tq, tk, cq=1024, hps=2):
    B, S, _ = q.shape
    nq, nk = S // tq, S // tk
    return pl.pallas_call(
        functools.partial(_mla_kernel, nk=nk, cq=cq),
        grid=(B, MLA_HEADS // hps, nq, nk),
        in_specs=[
            pl.BlockSpec((1, tq, hps * MLA_PAD), lambda b, h, i, j: (b, i, h)),
            pl.BlockSpec((1, tk, hps * MLA_PAD), lambda b, h, i, j: (b, j, h)),
            pl.BlockSpec((1, hps * MLA_V, tk), lambda b, h, i, j: (b, h, j)),
        ],
        out_specs=pl.BlockSpec((1, tq, hps * MLA_V), lambda b, h, i, j: (b, i, h)),
        out_shape=jax.ShapeDtypeStruct((B, S, MLA_HEADS * MLA_V), BF16),
        scratch_shapes=[
            pltpu.VMEM((hps, 1, tq), F32),
            pltpu.VMEM((hps, 2 * MLA_V, tq), F32),
        ],
        compiler_params=_cparams(("parallel", "parallel", "parallel", "arbitrary")),
    )(q, k, vt)


def _dil_kernel(prev_ref, cur_ref, next_ref, o_ref, lse_ref, *, nt, tt, nres):
    i = pl.program_id(2)
    nsub = tt // DIL_QB
    nkx = DIL_QB + 2 * DIL_STEPS
    qi = lax.broadcasted_iota(jnp.int32, (DIL_QB, nkx), 0)
    kx = lax.broadcasted_iota(jnp.int32, (DIL_QB, nkx), 1)
    rel = kx - DIL_STEPS - qi
    band = (rel >= -DIL_STEPS) & (rel <= DIL_STEPS)
    lane_q = lax.broadcasted_iota(jnp.int32, (DIL_QB, LANES), 1)
    first = lane_q < HEAD_DIM
    zero = jnp.zeros((DIL_QB, LANES), BF16)
    for res in range(nres):
        qc, kc, vc = (res * 3 * DIL_W + c * DIL_W for c in range(3))
        kext = jnp.concatenate([r[0, :, kc:kc + DIL_W] for r in (prev_ref, cur_ref, next_ref)], axis=0)
        vext = jnp.concatenate([r[0, :, vc:vc + DIL_W] for r in (prev_ref, cur_ref, next_ref)], axis=0)
        for j in range(nsub):
            ok = band
            if j == 0:
                ok = ok & ((kx >= DIL_STEPS) | (i > 0))
            if j == nsub - 1:
                ok = ok & ((kx < DIL_QB + DIL_STEPS) | (i < nt - 1))
            ok2 = jnp.concatenate([ok, ok], axis=0)
            rows = slice(j * DIL_QB, (j + 1) * DIL_QB)
            krows = slice(j * DIL_QB, j * DIL_QB + nkx)
            for hp in range(DIL_HEADS // 2):
                sl = slice(hp * LANES, (hp + 1) * LANES)
                qp = cur_ref[0, rows, qc + hp * LANES:qc + (hp + 1) * LANES]
                kp = kext[krows, sl]
                vp = vext[krows, sl]
                q2 = jnp.concatenate([jnp.where(first, qp, zero), jnp.where(first, zero, qp)], axis=0)
                s = lax.dot_general(q2, kp, (((1,), (1,)), ((), ())), preferred_element_type=F32)
                s = jnp.where(ok2, s, NEG_BIG)
                m = jnp.max(s, axis=1, keepdims=True)
                e = jnp.exp2(s - m)
                den = jnp.sum(e, axis=1, keepdims=True)
                o = jnp.dot(e.astype(BF16), vp, preferred_element_type=F32) / den
                lse = jnp.broadcast_to(m + jnp.log2(den), (2 * DIL_QB, LANES))
                oc = res * DIL_W + hp * LANES
                o_ref[0, rows, oc:oc + LANES] = jnp.where(first, o[:DIL_QB], o[DIL_QB:])
                lse_ref[0, rows, oc:oc + LANES] = jnp.where(first, lse[:DIL_QB], lse[DIL_QB:])


def _dil_attention(x, dil, tt):
    B, L, _ = x.shape
    nt = L // tt
    nres = max(1, min(dil, DIL_ROWS_PER_STEP // tt))
    hb = tt // DIL_STEPS
    nhb = L // DIL_STEPS
    w_in, w_out = nres * 3 * DIL_W, nres * DIL_W
    cur = pl.BlockSpec((1, tt, w_in), lambda b, r, i: (b, i, r))
    prev = pl.BlockSpec((1, DIL_STEPS, w_in), lambda b, r, i: (b, jnp.maximum(i * hb - 1, 0), r))
    nxt = pl.BlockSpec((1, DIL_STEPS, w_in), lambda b, r, i: (b, jnp.minimum((i + 1) * hb, nhb - 1), r))
    out_spec = pl.BlockSpec((1, tt, w_out), lambda b, r, i: (b, i, r))
    return pl.pallas_call(
        functools.partial(_dil_kernel, nt=nt, tt=tt, nres=nres),
        grid=(B, dil // nres, nt),
        in_specs=[prev, cur, nxt],
        out_specs=[out_spec, out_spec],
        out_shape=[jax.ShapeDtypeStruct((B, L, dil * DIL_W), F32)] * 2,
        compiler_params=_cparams(("parallel", "parallel", "parallel")),
    )(x, x, x)


def _ffn(x1, gf, wgu_ref, wdown_ref):
    h2 = _rms(x1, gf).astype(BF16)
    acc = x1
    for c in range(FF_CHUNKS):
        gate = jnp.dot(h2, wgu_ref[:, c * FF_CHUNK:(c + 1) * FF_CHUNK], preferred_element_type=F32)
        up = jnp.dot(h2, wgu_ref[:, D_FF + c * FF_CHUNK:D_FF + (c + 1) * FF_CHUNK],
                     preferred_element_type=F32)
        act = (gate * jax.nn.sigmoid(gate) * up).astype(BF16)
        acc = acc + jnp.dot(act, wdown_ref[c * FF_CHUNK:(c + 1) * FF_CHUNK, :], preferred_element_type=F32)
    return acc


def _out_ffn_attn_kernel(x_ref, om_ref, o0_ref, o1_ref, o2_ref, l0_ref, l1_ref, l2_ref,
                         wo_ref, gf_ref, wgu_ref, wdown_ref, y_ref, perm_scr):
    tm = x_ref.shape[0]

    def natural(ref, slot, dil):
        for r in range(dil):
            for b in range(DIL_W // LANES):
                perm_scr[slot, b, pl.ds(r, tm // dil, stride=dil), :] = ref[
                    :, r * DIL_W + b * LANES:r * DIL_W + (b + 1) * LANES]
        return jnp.concatenate([perm_scr[slot, b] for b in range(DIL_W // LANES)], axis=1)

    d1, d2 = DIL_PATTERNS[1][1], DIL_PATTERNS[2][1]
    l0, l1, l2 = l0_ref[...], natural(l1_ref, 0, d1), natural(l2_ref, 1, d2)
    mx = jnp.maximum(jnp.maximum(l0, l1), l2)
    e0, e1, e2 = jnp.exp2(l0 - mx), jnp.exp2(l1 - mx), jnp.exp2(l2 - mx)
    od = (e0 * o0_ref[...] + e1 * natural(o1_ref, 2, d1) + e2 * natural(o2_ref, 3, d2)) / (e0 + e1 + e2)
    o = jnp.concatenate([om_ref[...], od.astype(BF16)], axis=1)
    x1 = x_ref[...] + jnp.dot(o, wo_ref[...], preferred_element_type=F32)
    y_ref[...] = _ffn(x1, gf_ref[...], wgu_ref, wdown_ref)


def _out_ffn_rec_kernel(x_ref, a_ref, wo_ref, gf_ref, wgu_ref, wdown_ref, gfin_ref, y_ref):
    x1 = x_ref[...] + jnp.dot(a_ref[...], wo_ref[...], preferred_element_type=F32)
    y_ref[...] = _rms(_ffn(x1, gf_ref[...], wgu_ref, wdown_ref), gfin_ref[...])


def _out_ffn_attn(x, om, os_, ls_, wp, layer, tm):
    T, D = x.shape
    row = lambda w: pl.BlockSpec((tm, w), lambda i: (i, 0))
    grp = [pl.BlockSpec((tm // d, d * DIL_W), lambda i: (i, 0)) for _, d in DIL_PATTERNS]
    return pl.pallas_call(
        _out_ffn_attn_kernel,
        grid=(T // tm,),
        in_specs=[row(D), row(DIL_W)] + grp + grp + [
            _const_spec((MLA_HEADS * MLA_V + DIL_W, D)), _const_spec((1, D)),
            _const_spec((D, 2 * D_FF)), _const_spec((D_FF, D))],
        out_specs=row(D),
        out_shape=jax.ShapeDtypeStruct((T, D), F32),
        scratch_shapes=[pltpu.VMEM((4, DIL_W // LANES, tm, LANES), F32)],
        compiler_params=_cparams(("parallel",)),
    )(x, om, *os_, *ls_, wp["w_out_a"], wp["norm_ffn"][layer],
      wp["w_gu"][layer], wp["w_down"][layer])


def _out_ffn_rec(x, a, wp, layer, tm):
    T, D = x.shape
    row = lambda w: pl.BlockSpec((tm, w), lambda i: (i, 0))
    return pl.pallas_call(
        _out_ffn_rec_kernel,
        grid=(T // tm,),
        in_specs=[row(D), row(D_RNN), _const_spec((D_RNN, D)), _const_spec((1, D)),
                  _const_spec((D, 2 * D_FF)), _const_spec((D_FF, D)), _const_spec((1, D))],
        out_specs=row(D),
        out_shape=jax.ShapeDtypeStruct((T, D), F32),
        compiler_params=_cparams(("parallel",)),
    )(x, a, wp["w_out_r"], wp["norm_ffn"][layer], wp["w_gu"][layer], wp["w_down"][layer],
      wp["norm_final"])


def _l1_prep_kernel(x_ref, g_ref, w_ref, y_ref, xr_ref):
    h = _rms(x_ref[...], g_ref[...]).astype(BF16)
    zy = jnp.dot(h, w_ref[:, :D_RNN], preferred_element_type=F32)
    y_ref[...] = (0.5 * zy * (1.0 + jnp.tanh(np.sqrt(2.0 / np.pi).astype(np.float32)
                                             * (zy + 0.044715 * (zy * zy * zy))))).astype(BF16)
    xr_ref[...] = jnp.dot(h, w_ref[:, D_RNN:], preferred_element_type=F32)


def _l1_prep(x, g, w, tm):
    T, D = x.shape
    row = lambda w_: pl.BlockSpec((tm, w_), lambda i: (i, 0))
    return pl.pallas_call(
        _l1_prep_kernel,
        grid=(T // tm,),
        in_specs=[row(D), _const_spec((1, D)), _const_spec((D, 2 * D_RNN))],
        out_specs=[row(D_RNN), row(D_RNN)],
        out_shape=[jax.ShapeDtypeStruct((T, D_RNN), BF16), jax.ShapeDtypeStruct((T, D_RNN), F32)],
        compiler_params=_cparams(("parallel",)),
    )(x, g, w)


def _rglru_kernel(*refs, ns, ts, cb, reverse):
    if reverse:
        (xconv_ref, wg_ref, br_ref, bi_ref, lam_ref, hf_ref, y_ref,
         o_ref, a_scr, u_scr, h_scr, carry_scr) = refs
    else:
        (xp_ref, xc_ref, xn_ref, cw_ref, cbias_ref, wg_ref, br_ref, bi_ref, lam_ref,
         o_ref, xconv_ref, a_scr, u_scr, carry_scr) = refs
        h_scr = o_ref.at[0]
    step = pl.program_id(2)

    @pl.when(step == 0)
    def _():
        carry_scr[...] = jnp.zeros(carry_scr.shape, F32)

    if reverse:
        xc = xconv_ref[0]
    else:
        x0 = xc_ref[0]
        rows = lax.broadcasted_iota(jnp.int32, (ts, cb), 0)
        pv = jnp.where(step > 0, xp_ref[0], 0.0)
        nv = jnp.where(step < ns - 1, xn_ref[0], 0.0)
        xm1 = jnp.where(rows == 0, pv[SUBLANES - 1:SUBLANES], pltpu.roll(x0, 1, 0))
        xm2 = pltpu.roll(x0, 2, 0)
        xm2 = jnp.where(rows == 0, pv[SUBLANES - 2:SUBLANES - 1], xm2)
        xm2 = jnp.where(rows == 1, pv[SUBLANES - 1:SUBLANES], xm2)
        xp1 = jnp.where(rows == ts - 1, nv[0:1], pltpu.roll(x0, ts - 1, 0))
        xc = (xm2 * cw_ref[0:1] + xm1 * cw_ref[1:2] + x0 * cw_ref[2:3] + xp1 * cw_ref[3:4]) + cbias_ref[...]
        xconv_ref[0] = xc

    xcb = xc.astype(BF16)
    for j in range(cb // LRU_BW):
        sl = slice(j * LRU_BW, (j + 1) * LRU_BW)
        gts = jnp.dot(xcb[:, sl], wg_ref[j], preferred_element_type=F32)
        tr = jnp.tanh(gts[:, :LRU_BW] + br_ref[:, sl])
        ti = jnp.tanh(gts[:, LRU_BW:] + bi_ref[:, sl])
        nlam = -lam_ref[:, sl]
        softplus = jnp.maximum(nlam, 0.0) + jnp.log1p(jnp.exp(-jnp.abs(nlam)))
        c = (-0.5 * LRU_C) * softplus
        a = jnp.exp(c * tr + c)
        a_scr[:, sl] = a
        x = 1.0 - a * a
        half_root = 0.5 * jnp.where(x > 0.0, x * lax.rsqrt(x), 0.0)
        u_scr[:, sl] = half_root * ((ti + 1.0) * xc[:, sl])

    row8 = lax.broadcasted_iota(jnp.int32, (SUBLANES, cb), 0)
    nchunk = ts // SUBLANES

    def chunk(c, hprev):
        idx = (nchunk - 1 - c) if reverse else c
        r0 = pl.multiple_of(idx * SUBLANES, SUBLANES)
        a = a_scr[pl.ds(r0, SUBLANES), :]
        b = u_scr[pl.ds(r0, SUBLANES), :]
        for d in (1, 2, 4):
            if reverse:
                keep = row8 < SUBLANES - d
                sh = SUBLANES - d
            else:
                keep = row8 >= d
                sh = d
            a_s = jnp.where(keep, pltpu.roll(a, sh, 0), 1.0)
            b_s = jnp.where(keep, pltpu.roll(b, sh, 0), 0.0)
            b = a * b_s + b
            a = a * a_s
        h = a * hprev + b
        h_scr[pl.ds(r0, SUBLANES), :] = h
        edge = h[0:1] if reverse else h[SUBLANES - 1:SUBLANES]
        return jnp.broadcast_to(edge, (SUBLANES, cb))

    carry_scr[...] = lax.fori_loop(0, nchunk, chunk, carry_scr[...], unroll=4)

    if reverse:
        o_ref[0] = ((hf_ref[0] + h_scr[...]) * y_ref[0].astype(F32)).astype(BF16)


def _rglru(xr, wp, ts, cb, reverse, hf=None, y=None):
    B, S, C = xr.shape
    ns = S // ts
    nhb = S // SUBLANES
    hb = ts // SUBLANES
    d = 1 if reverse else 0
    tidx = (lambda i: ns - 1 - i) if reverse else (lambda i: i)
    cur = pl.BlockSpec((1, ts, cb), lambda b, c, i: (b, tidx(i), c))
    vec = pl.BlockSpec((1, cb), lambda b, c, i: (0, c))
    gate_specs = [pl.BlockSpec((cb // LRU_BW, LRU_BW, 2 * LRU_BW), lambda b, c, i: (c, 0, 0)), vec, vec, vec]
    gate_args = [wp["w_gate"][d], wp["b_r"][d], wp["b_i"][d], wp["lam"][d]]
    tile = pltpu.VMEM((ts, cb), F32)
    carry = pltpu.VMEM((SUBLANES, cb), F32)
    if reverse:
        in_specs = [cur] + gate_specs + [cur, cur]
        args = [xr] + gate_args + [hf, y]
        scratch = [tile, tile, tile, carry]
        out_specs, out_shape = cur, jax.ShapeDtypeStruct((B, S, C), BF16)
    else:
        prev = pl.BlockSpec((1, SUBLANES, cb), lambda b, c, i: (b, jnp.maximum(i * hb - 1, 0), c))
        nxt = pl.BlockSpec((1, SUBLANES, cb), lambda b, c, i: (b, jnp.minimum((i + 1) * hb, nhb - 1), c))
        in_specs = [prev, cur, nxt, pl.BlockSpec((CONV_W, cb), lambda b, c, i: (0, c)), vec] + gate_specs
        args = [xr, xr, xr, wp["conv_w"], wp["conv_b"]] + gate_args
        scratch = [tile, tile, carry]
        out_specs, out_shape = [cur, cur], [jax.ShapeDtypeStruct((B, S, C), F32)] * 2
    return pl.pallas_call(
        functools.partial(_rglru_kernel, ns=ns, ts=ts, cb=cb, reverse=reverse),
        grid=(B, C // cb, ns),
        in_specs=in_specs,
        out_specs=out_specs,
        out_shape=out_shape,
        scratch_shapes=scratch,
        compiler_params=_cparams(("parallel", "parallel", "arbitrary")),
    )(*args)


def _lane_table(vals, jidx, mask, fill):
    return jnp.where(mask[None, :], vals[:, jidx], fill)


def _rope_tables(S, half, period, start):
    inv = jnp.power(ROPE_THETA, -jnp.arange(half, dtype=F32) / half)
    ang = jnp.arange(S, dtype=F32)[:, None] * inv[None, :]
    cos, sin = jnp.cos(ang), jnp.sin(ang)
    e = np.arange(LANES) % period - start
    first = (e >= 0) & (e < half)
    second = (e >= half) & (e < 2 * half)
    jidx = np.where(first | second, e % half, 0)
    c = _lane_table(cos, jidx, first | second, 1.0)
    s1 = _lane_table(-sin, jidx, first, 0.0)
    s2 = _lane_table(sin, jidx, second, 0.0)
    return jnp.stack([c, s1, s2])


def _all_tables(S):
    return jnp.concatenate([
        _rope_tables(S, MLA_ROPE // 2, MLA_PAD, MLA_NOPE),
        _rope_tables(S, DIL_ROT // 2, HEAD_DIM, 0),
    ])


def _prep_weights(norm_mix, w_in_a, q_norm, w_uq, kv_norm, w_ukv, w_out_a, w_in_r, conv_w, conv_b,
                  lru_w_gate, lru_b_gate, lru_lambda, w_out_r, norm_ffn, w_gu, w_down, norm_final):
    w_in = w_in_a[0]
    qk_dim = MLA_NOPE + MLA_ROPE
    kr_cols = jnp.pad(w_in[:, MLA_Q_RANK + MLA_KV_RANK:MLA_IN],
                      ((0, 0), (MLA_NOPE, MLA_PAD - qk_dim)))
    w_mla = jnp.concatenate([w_in[:, :MLA_Q_RANK + MLA_KV_RANK], kr_cols], axis=1)
    uq = w_uq[0].reshape(MLA_Q_RANK, MLA_HEADS, qk_dim)
    uq = jnp.pad(uq, ((0, 0), (0, 0), (0, MLA_PAD - qk_dim))).reshape(MLA_Q_RANK, MLA_HEADS * MLA_PAD)
    ukv = w_ukv[0].reshape(MLA_KV_RANK, MLA_HEADS, MLA_NOPE + MLA_V)
    uk = jnp.pad(ukv[:, :, :MLA_NOPE], ((0, 0), (0, 0), (0, MLA_PAD - MLA_NOPE)))
    uk = uk.reshape(MLA_KV_RANK, MLA_HEADS * MLA_PAD)
    uv = ukv[:, :, MLA_NOPE:].reshape(MLA_KV_RANK, MLA_HEADS * MLA_V).T
    wg = lru_w_gate[0]
    w_gate = jnp.concatenate([wg[:, 0], wg[:, 1]], axis=-1)
    row = lambda v: v.reshape(1, -1)
    return {
        "norm_mix": [row(norm_mix[l]) for l in range(2)],
        "w_mla": w_mla.astype(BF16),
        "w_dil": w_in[:, MLA_IN:].astype(BF16),
        "q_norm": row(q_norm[0]),
        "w_uq": uq.astype(BF16),
        "kv_norm": row(kv_norm[0]),
        "w_uk": uk.astype(BF16),
        "w_uv": uv.astype(BF16),
        "w_out_a": w_out_a[0].astype(BF16),
        "w_in_r": w_in_r[0].astype(BF16),
        "conv_w": conv_w[0],
        "conv_b": row(conv_b[0]),
        "w_gate": (0.5 * w_gate).astype(BF16),
        "b_r": [row(0.5 * lru_b_gate[0, d, 0]) for d in range(2)],
        "b_i": [row(0.5 * lru_b_gate[0, d, 1]) for d in range(2)],
        "lam": [row(lru_lambda[0, d]) for d in range(2)],
        "w_out_r": w_out_r[0].astype(BF16),
        "norm_ffn": [row(norm_ffn[l]) for l in range(2)],
        "w_gu": [w_gu[l].astype(BF16) for l in range(2)],
        "w_down": [w_down[l].astype(BF16) for l in range(2)],
        "norm_final": row(norm_final),
    }


def _trunk(x, wp):
    B, S, D = x.shape
    T = B * S
    tabs = _all_tables(S)
    q, k, vt, *dgs = _l0_prep(x, wp["norm_mix"][0], wp, tabs, tm=512)
    tq, cq = min(S, 8192), 1024
    o_mla = _mla_attention(q, k, vt, tq=tq, tk=min(S, 2048), cq=cq, hps=min(4, max(2, 16 * cq // tq)))
    os_, ls_ = [], []
    for dg, (_, dil) in zip(dgs, DIL_PATTERNS):
        o_g, l_g = _dil_attention(dg, dil, tt=min(S // dil, 512))
        os_.append(o_g.reshape(T // dil, dil * DIL_W))
        ls_.append(l_g.reshape(T // dil, dil * DIL_W))
    x1 = _out_ffn_attn(x.reshape(T, D), o_mla.reshape(T, MLA_HEADS * MLA_V), os_, ls_, wp, 0, tm=512)
    y, xr = _l1_prep(x1, wp["norm_mix"][1], wp["w_in_r"], tm=512)
    xr = xr.reshape(B, S, D_RNN)
    hf, xconv = _rglru(xr, wp, ts=512, cb=D_RNN, reverse=False)
    a = _rglru(xconv, wp, ts=512, cb=D_RNN, reverse=True, hf=hf, y=y.reshape(B, S, D_RNN))
    out = _out_ffn_rec(x1, a.reshape(T, D_RNN), wp, 1, tm=512)
    return out.reshape(B, S, D)


def kernel(x_prompt, x_sample, norm_mix, w_in_a, q_norm, w_uq, kv_norm, w_ukv, w_out_a, w_in_r, conv_w,
           conv_b, lru_w_gate, lru_b_gate, lru_lambda, w_out_r, norm_ffn, w_gu, w_down, norm_final):
    wp = _prep_weights(norm_mix, w_in_a, q_norm, w_uq, kv_norm, w_ukv, w_out_a, w_in_r, conv_w, conv_b,
                       lru_w_gate, lru_b_gate, lru_lambda, w_out_r, norm_ffn, w_gu, w_down, norm_final)
    return (_trunk(x_prompt, wp), _trunk(x_sample, wp))
```

```python
import functools

import numpy as np
import jax
import jax.numpy as jnp
from jax import lax
from jax.experimental import pallas as pl
from jax.experimental.pallas import tpu as pltpu

F32 = jnp.float32
BF16 = jnp.bfloat16

D_MODEL = 1024
HEAD_DIM = 64
ROPE_THETA = 500000.0
NORM_EPS = 1e-6
MLA_HEADS = 8
MLA_NOPE = 64
MLA_ROPE = 32
MLA_V = 64
MLA_Q_RANK = 256
MLA_KV_RANK = 128
DIL_PATTERNS = ((128, 1), (512, 4), (2048, 16))
DIL_GROUPS = len(DIL_PATTERNS)
DIL_HEADS = 8
DIL_ROT = HEAD_DIM // 4
DIL_STEPS = 64
DIL_QB = 128
DIL_ROWS_PER_STEP = 512
MLA_IN = MLA_Q_RANK + MLA_KV_RANK + MLA_ROPE
DIL_W = DIL_HEADS * HEAD_DIM
DIL_QKV = 3 * DIL_GROUPS * DIL_W
D_RNN = 1536
LRU_BLOCKS = 12
LRU_BW = D_RNN // LRU_BLOCKS
CONV_W = 4
CONV_LEFT = 2
LRU_C = 8.0
D_FF = ((8 * D_MODEL // 3 + 255) // 256) * 256
NEG_BIG = -1e30
MLA_SCALE = float((MLA_NOPE + MLA_ROPE) ** -0.5 * np.log2(np.e))
DIL_SCALE = float(HEAD_DIM ** -0.5 * np.log2(np.e))

LANES = 128
SUBLANES = 8
MLA_PAD = 128
VMEM_LIMIT = 56 * 1024 * 1024

FF_CHUNK = 256
FF_CHUNKS = D_FF // FF_CHUNK


def _cparams(sem):
    return pltpu.CompilerParams(dimension_semantics=sem, vmem_limit_bytes=VMEM_LIMIT)


def _const_spec(shape):
    nd = len(shape)
    return pl.BlockSpec(shape, lambda *_: (0,) * nd, pipeline_mode=pl.Buffered(1))


def _rms(x, g):
    return x * lax.rsqrt(jnp.mean(x * x, axis=-1, keepdims=True) + NORM_EPS) * g


def _rope_block(xb, c, s1, s2, half):
    return xb * c + pltpu.roll(xb, LANES - half, 1) * s1 + pltpu.roll(xb, half, 1) * s2


def _l0_prep_kernel(x_ref, g_ref, wmla_ref, wdil_ref, qn_ref, wuq_ref, kvn_ref, wuk_ref, wuv_ref,
                    tab_ref, q_out, k_out, v_out, d0_out, d1_out, d2_out, perm_scr):
    tm = x_ref.shape[1]
    d_outs = (d0_out, d1_out, d2_out)
    x = x_ref[0]
    h = _rms(x, g_ref[...]).astype(BF16)
    z = jnp.dot(h, wmla_ref[...], preferred_element_type=F32)
    qn = _rms(z[:, :MLA_Q_RANK], qn_ref[...]).astype(BF16)
    kvn = _rms(z[:, MLA_Q_RANK:MLA_Q_RANK + MLA_KV_RANK], kvn_ref[...]).astype(BF16)
    kr = z[:, MLA_Q_RANK + MLA_KV_RANK:]
    qf = jnp.dot(qn, wuq_ref[...], preferred_element_type=F32)
    kf = jnp.dot(kvn, wuk_ref[...], preferred_element_type=F32)
    vt = lax.dot_general(wuv_ref[...], kvn, (((1,), (1,)), ((), ())),
                         preferred_element_type=F32)
    half = MLA_ROPE // 2
    krr = _rope_block(kr, tab_ref[0], tab_ref[1], tab_ref[2], half)
    for hh in range(MLA_HEADS):
        sl = slice(hh * MLA_PAD, (hh + 1) * MLA_PAD)
        q_out[0, :, sl] = _rope_block(qf[:, sl] * MLA_SCALE, tab_ref[0], tab_ref[1], tab_ref[2], half).astype(BF16)
        k_out[0, :, sl] = (kf[:, sl] + krr).astype(BF16)
    v_out[0] = vt.astype(BF16)
    dhalf = DIL_ROT // 2
    for j in range(3 * DIL_GROUPS):
        c, g = divmod(j, DIL_GROUPS)
        dil = DIL_PATTERNS[g][1]
        zc = jnp.dot(h, wdil_ref[:, j * DIL_W:(j + 1) * DIL_W], preferred_element_type=F32)
        for b in range(DIL_W // LANES):
            zb = zc[:, b * LANES:(b + 1) * LANES]
            if c == 0:
                zb = zb * DIL_SCALE
            if c < 2:
                zb = _rope_block(zb, tab_ref[3], tab_ref[4], tab_ref[5], dhalf)
            if dil == 1:
                d_outs[g][0, :, c * DIL_W + b * LANES:c * DIL_W + (b + 1) * LANES] = zb.astype(BF16)
            else:
                perm_scr[b] = zb
        if dil > 1:
            for r in range(dil):
                for b in range(DIL_W // LANES):
                    col = (r * 3 + c) * DIL_W + b * LANES
                    d_outs[g][0, :, col:col + LANES] = perm_scr[
                        b, pl.ds(r, tm // dil, stride=dil), :].astype(BF16)


def _l0_prep(x, g, wp, tabs, tm):
    B, S, D = x.shape
    grid = (B, S // tm)
    row = lambda w: pl.BlockSpec((1, tm, w), lambda b, i: (b, i, 0))
    return pl.pallas_call(
        _l0_prep_kernel,
        grid=grid,
        in_specs=[
            row(D),
            _const_spec((1, D)),
            _const_spec(wp["w_mla"].shape),
            _const_spec(wp["w_dil"].shape),
            _const_spec((1, MLA_Q_RANK)),
            _const_spec(wp["w_uq"].shape),
            _const_spec((1, MLA_KV_RANK)),
            _const_spec(wp["w_uk"].shape),
            _const_spec(wp["w_uv"].shape),
            pl.BlockSpec((6, tm, LANES), lambda b, i: (0, i, 0)),
        ],
        out_specs=[row(MLA_HEADS * MLA_PAD), row(MLA_HEADS * MLA_PAD),
                   pl.BlockSpec((1, MLA_HEADS * MLA_V, tm), lambda b, i: (b, 0, i))] + [
            pl.BlockSpec((1, tm // d, d * 3 * DIL_W), lambda b, i: (b, i, 0)) for _, d in DIL_PATTERNS],
        out_shape=[
            jax.ShapeDtypeStruct((B, S, MLA_HEADS * MLA_PAD), BF16),
            jax.ShapeDtypeStruct((B, S, MLA_HEADS * MLA_PAD), BF16),
            jax.ShapeDtypeStruct((B, MLA_HEADS * MLA_V, S), BF16),
        ] + [jax.ShapeDtypeStruct((B, S // d, d * 3 * DIL_W), BF16) for _, d in DIL_PATTERNS],
        scratch_shapes=[pltpu.VMEM((DIL_W // LANES, tm, LANES), F32)],
        compiler_params=_cparams(("parallel", "parallel")),
    )(x, g, wp["w_mla"], wp["w_dil"], wp["q_norm"], wp["w_uq"], wp["kv_norm"], wp["w_uk"], wp["w_uv"], tabs)


def _mla_kernel(q_ref, k_ref, vt_ref, o_ref, m_scr, acc_scr, *, nk, cq):
    ki = pl.program_id(3)

    @pl.when(ki == 0)
    def _():
        m_scr[...] = jnp.full(m_scr.shape, -jnp.inf, F32)
        acc_scr[...] = jnp.zeros(acc_scr.shape, F32)

    tq = q_ref.shape[1]
    hps = m_scr.shape[0]
    pair_shape = (2 * MLA_V, vt_ref.shape[2])
    own_rows = lax.broadcasted_iota(jnp.int32, pair_shape, 0) < MLA_V
    ones = jnp.ones(pair_shape, BF16)
    v1 = []
    for pair in range(hps // 2):
        vt = vt_ref[0, pair * 2 * MLA_V:(pair + 1) * 2 * MLA_V, :]
        v1 += [jnp.where(own_rows, vt, ones), jnp.where(own_rows, ones, vt)]

    def scores(hh, c):
        q = q_ref[0, c * cq:(c + 1) * cq, hh * MLA_PAD:(hh + 1) * MLA_PAD]
        k = k_ref[0, :, hh * MLA_PAD:(hh + 1) * MLA_PAD]
        st = lax.dot_general(k, q, (((1,), (1,)), ((), ())), preferred_element_type=F32)
        m_prev = m_scr[hh, :, c * cq:(c + 1) * cq]
        m_new = jnp.maximum(m_prev, jnp.max(st, axis=0, keepdims=True))
        m_scr[hh, :, c * cq:(c + 1) * cq] = m_new
        return st, m_prev, m_new

    def accumulate(hh, c, st, m_prev, m_new):
        alpha = jnp.exp2(m_prev - m_new)
        p = jnp.exp2(st - m_new).astype(BF16)
        acc_scr[hh, :, c * cq:(c + 1) * cq] = alpha * acc_scr[hh, :, c * cq:(c + 1) * cq] + jnp.dot(
            v1[hh], p, preferred_element_type=F32)

    tasks = [(hh, c) for hh in range(hps) for c in range(tq // cq)]
    pending = scores(*tasks[0])
    for t, task in enumerate(tasks):
        nxt = scores(*tasks[t + 1]) if t + 1 < len(tasks) else None
        accumulate(*task, *pending)
        pending = nxt

    @pl.when(ki == nk - 1)
    def _():
        for pair in range(hps // 2):
            a0, a1 = acc_scr[2 * pair], acc_scr[2 * pair + 1]
            ot = jnp.concatenate([a0[:MLA_V] / a0[MLA_V:MLA_V + 1], a1[MLA_V:] / a1[0:1]], axis=0)
            o_ref[0, :, pair * 2 * MLA_V:(pair + 1) * 2 * MLA_V] = ot.T.astype(BF16)


def _mla_attention(q, k, vt, tq, tk, cq=1024, hps=2):
    B, S, _ = q.shape
    nq, nk = S // tq, S // tk
    return pl.pallas_call(
        functools.partial(_mla_kernel, nk=nk, cq=cq),
        grid=(B, MLA_HEADS // hps, nq, nk),
        in_specs=[
            pl.BlockSpec((1, tq, hps * MLA_PAD), lambda b, h, i, j: (b, i, h)),
            pl.BlockSpec((1, tk, hps * MLA_PAD), lambda b, h, i, j: (b, j, h)),
            pl.BlockSpec((1, hps * MLA_V, tk), lambda b, h, i, j: (b, h, j)),
        ],
        out_specs=pl.BlockSpec((1, tq, hps * MLA_V), lambda b, h, i, j: (b, i, h)),
        out_shape=jax.ShapeDtypeStruct((B, S, MLA_HEADS * MLA_V), BF16),
        scratch_shapes=[
            pltpu.VMEM((hps, 1, tq), F32),
            pltpu.VMEM((hps, 2 * MLA_V, tq), F32),
        ],
        compiler_params=_cparams(("parallel", "parallel", "parallel", "arbitrary")),
    )(q, k, vt)


def _dil_kernel(prev_ref, cur_ref, next_ref, o_ref, lse_ref, *, nt, tt, nres):
    i = pl.program_id(2)
    nsub = tt // DIL_QB
    nkx = DIL_QB + 2 * DIL_STEPS
    qi = lax.broadcasted_iota(jnp.int32, (DIL_QB, nkx), 0)
    kx = lax.broadcasted_iota(jnp.int32, (DIL_QB, nkx), 1)
    rel = kx - DIL_STEPS - qi
    band = (rel >= -DIL_STEPS) & (rel <= DIL_STEPS)
    lane_q = lax.broadcasted_iota(jnp.int32, (DIL_QB, LANES), 1)
    first = lane_q < HEAD_DIM
    zero = jnp.zeros((DIL_QB, LANES), BF16)
    ones_kv = jnp.ones((nkx, LANES), BF16)
    for res in range(nres):
        qc, kc, vc = (res * 3 * DIL_W + c * DIL_W for c in range(3))
        kext = jnp.concatenate([r[0, :, kc:kc + DIL_W] for r in (prev_ref, cur_ref, next_ref)], axis=0)
        vext = jnp.concatenate([r[0, :, vc:vc + DIL_W] for r in (prev_ref, cur_ref, next_ref)], axis=0)
        for j in range(nsub):
            ok = band
            if j == 0:
                ok = ok & ((kx >= DIL_STEPS) | (i > 0))
            if j == nsub - 1:
                ok = ok & ((kx < DIL_QB + DIL_STEPS) | (i < nt - 1))
            ok2 = jnp.concatenate([ok, ok], axis=0)
            rows = slice(j * DIL_QB, (j + 1) * DIL_QB)
            krows = slice(j * DIL_QB, j * DIL_QB + nkx)
            for hp in range(DIL_HEADS // 2):
                sl = slice(hp * LANES, (hp + 1) * LANES)
                qp = cur_ref[0, rows, qc + hp * LANES:qc + (hp + 1) * LANES]
                kp = kext[krows, sl]
                vp = vext[krows, sl]
                q2 = jnp.concatenate([jnp.where(first, qp, zero), jnp.where(first, zero, qp)], axis=0)
                s = lax.dot_general(q2, kp, (((1,), (1,)), ((), ())), preferred_element_type=F32)
                s = jnp.where(ok2, s, NEG_BIG)
                m = jnp.max(s, axis=1, keepdims=True)
                e = jnp.exp2(s - m).astype(BF16)
                oe = jnp.dot(e, jnp.concatenate([vp, ones_kv], axis=1), preferred_element_type=F32)
                den = oe[:, LANES:]
                o = oe[:, :LANES] / den
                lse = m + jnp.log2(den)
                oc = res * DIL_W + hp * LANES
                o_ref[0, rows, oc:oc + LANES] = jnp.where(first, o[:DIL_QB], o[DIL_QB:])
                lse_ref[0, rows, oc:oc + LANES] = jnp.where(first, lse[:DIL_QB], lse[DIL_QB:])


def _dil_attention(x, dil, tt):
    B, L, _ = x.shape
    nt = L // tt
    nres = max(1, min(dil, DIL_ROWS_PER_STEP // tt))
    hb = tt // DIL_STEPS
    nhb = L // DIL_STEPS
    w_in, w_out = nres * 3 * DIL_W, nres * DIL_W
    cur = pl.BlockSpec((1, tt, w_in), lambda b, r, i: (b, i, r))
    prev = pl.BlockSpec((1, DIL_STEPS, w_in), lambda b, r, i: (b, jnp.maximum(i * hb - 1, 0), r))
    nxt = pl.BlockSpec((1, DIL_STEPS, w_in), lambda b, r, i: (b, jnp.minimum((i + 1) * hb, nhb - 1), r))
    out_spec = pl.BlockSpec((1, tt, w_out), lambda b, r, i: (b, i, r))
    return pl.pallas_call(
        functools.partial(_dil_kernel, nt=nt, tt=tt, nres=nres),
        grid=(B, dil // nres, nt),
        in_specs=[prev, cur, nxt],
        out_specs=[out_spec, out_spec],
        out_shape=[jax.ShapeDtypeStruct((B, L, dil * DIL_W), F32)] * 2,
        compiler_params=_cparams(("parallel", "parallel", "parallel")),
    )(x, x, x)


def _ffn(x1, gf, wgu_ref, wdown_ref):
    h2 = _rms(x1, gf).astype(BF16)
    acc = x1
    for c in range(FF_CHUNKS):
        gate = jnp.dot(h2, wgu_ref[:, c * FF_CHUNK:(c + 1) * FF_CHUNK], preferred_element_type=F32)
        up = jnp.dot(h2, wgu_ref[:, D_FF + c * FF_CHUNK:D_FF + (c + 1) * FF_CHUNK],
                     preferred_element_type=F32)
        act = (gate * jax.nn.sigmoid(gate) * up).astype(BF16)
        acc = acc + jnp.dot(act, wdown_ref[c * FF_CHUNK:(c + 1) * FF_CHUNK, :], preferred_element_type=F32)
    return acc


def _out_ffn_attn_kernel(x_ref, om_ref, o0_ref, o1_ref, o2_ref, l0_ref, l1_ref, l2_ref,
                         wo_ref, gf_ref, wgu_ref, wdown_ref, y_ref, perm_scr):
    tm = x_ref.shape[0]

    def natural(ref, slot, dil):
        for r in range(dil):
            for b in range(DIL_W // LANES):
                perm_scr[slot, b, pl.ds(r, tm // dil, stride=dil), :] = ref[
                    :, r * DIL_W + b * LANES:r * DIL_W + (b + 1) * LANES]
        return jnp.concatenate([perm_scr[slot, b] for b in range(DIL_W // LANES)], axis=1)

    d1, d2 = DIL_PATTERNS[1][1], DIL_PATTERNS[2][1]
    l0, l1, l2 = l0_ref[...], natural(l1_ref, 0, d1), natural(l2_ref, 1, d2)
    mx = jnp.maximum(jnp.maximum(l0, l1), l2)
    e0, e1, e2 = jnp.exp2(l0 - mx), jnp.exp2(l1 - mx), jnp.exp2(l2 - mx)
    od = (e0 * o0_ref[...] + e1 * natural(o1_ref, 2, d1) + e2 * natural(o2_ref, 3, d2)) / (e0 + e1 + e2)
    o = jnp.concatenate([om_ref[...], od.astype(BF16)], axis=1)
    x1 = x_ref[...] + jnp.dot(o, wo_ref[...], preferred_element_type=F32)
    y_ref[...] = _ffn(x1, gf_ref[...], wgu_ref, wdown_ref)


def _out_ffn_rec_kernel(x_ref, a_ref, wo_ref, gf_ref, wgu_ref, wdown_ref, gfin_ref, y_ref):
    x1 = x_ref[...] + jnp.dot(a_ref[...], wo_ref[...], preferred_element_type=F32)
    y_ref[...] = _rms(_ffn(x1, gf_ref[...], wgu_ref, wdown_ref), gfin_ref[...])


def _out_ffn_attn(x, om, os_, ls_, wp, layer, tm):
    T, D = x.shape
    row = lambda w: pl.BlockSpec((tm, w), lambda i: (i, 0))
    grp = [pl.BlockSpec((tm // d, d * DIL_W), lambda i: (i, 0)) for _, d in DIL_PATTERNS]
    return pl.pallas_call(
        _out_ffn_attn_kernel,
        grid=(T // tm,),
        in_specs=[row(D), row(DIL_W)] + grp + grp + [
            _const_spec((MLA_HEADS * MLA_V + DIL_W, D)), _const_spec((1, D)),
            _const_spec((D, 2 * D_FF)), _const_spec((D_FF, D))],
        out_specs=row(D),
        out_shape=jax.ShapeDtypeStruct((T, D), F32),
        scratch_shapes=[pltpu.VMEM((4, DIL_W // LANES, tm, LANES), F32)],
        compiler_params=_cparams(("parallel",)),
    )(x, om, *os_, *ls_, wp["w_out_a"], wp["norm_ffn"][layer],
      wp["w_gu"][layer], wp["w_down"][layer])


def _out_ffn_rec(x, a, wp, layer, tm):
    T, D = x.shape
    row = lambda w: pl.BlockSpec((tm, w), lambda i: (i, 0))
    return pl.pallas_call(
        _out_ffn_rec_kernel,
        grid=(T // tm,),
        in_specs=[row(D), row(D_RNN), _const_spec((D_RNN, D)), _const_spec((1, D)),
                  _const_spec((D, 2 * D_FF)), _const_spec((D_FF, D)), _const_spec((1, D))],
        out_specs=row(D),
        out_shape=jax.ShapeDtypeStruct((T, D), F32),
        compiler_params=_cparams(("parallel",)),
    )(x, a, wp["w_out_r"], wp["norm_ffn"][layer], wp["w_gu"][layer], wp["w_down"][layer],
      wp["norm_final"])


def _l1_prep_kernel(x_ref, g_ref, w_ref, y_ref, xr_ref):
    h = _rms(x_ref[...], g_ref[...]).astype(BF16)
    zy = jnp.dot(h, w_ref[:, :D_RNN], preferred_element_type=F32)
    y_ref[...] = (0.5 * zy * (1.0 + jnp.tanh(np.sqrt(2.0 / np.pi).astype(np.float32)
                                             * (zy + 0.044715 * (zy * zy * zy))))).astype(BF16)
    xr_ref[...] = jnp.dot(h, w_ref[:, D_RNN:], preferred_element_type=F32)


def _l1_prep(x, g, w, tm):
    T, D = x.shape
    row = lambda w_: pl.BlockSpec((tm, w_), lambda i: (i, 0))
    return pl.pallas_call(
        _l1_prep_kernel,
        grid=(T // tm,),
        in_specs=[row(D), _const_spec((1, D)), _const_spec((D, 2 * D_RNN))],
        out_specs=[row(D_RNN), row(D_RNN)],
        out_shape=[jax.ShapeDtypeStruct((T, D_RNN), BF16), jax.ShapeDtypeStruct((T, D_RNN), F32)],
        compiler_params=_cparams(("parallel",)),
    )(x, g, w)


def _rglru_kernel(*refs, ns, ts, cb, reverse):
    if reverse:
        (xconv_ref, wg_ref, br_ref, bi_ref, lam_ref, hf_ref, y_ref,
         o_ref, a_scr, u_scr, h_scr, carry_scr) = refs
    else:
        (xp_ref, xc_ref, xn_ref, cw_ref, cbias_ref, wg_ref, br_ref, bi_ref, lam_ref,
         o_ref, xconv_ref, a_scr, u_scr, carry_scr) = refs
        h_scr = o_ref.at[0]
    step = pl.program_id(2)

    @pl.when(step == 0)
    def _():
        carry_scr[...] = jnp.zeros(carry_scr.shape, F32)

    if reverse:
        xc = xconv_ref[0]
    else:
        x0 = xc_ref[0]
        rows = lax.broadcasted_iota(jnp.int32, (ts, cb), 0)
        pv = jnp.where(step > 0, xp_ref[0], 0.0)
        nv = jnp.where(step < ns - 1, xn_ref[0], 0.0)
        xm1 = jnp.where(rows == 0, pv[SUBLANES - 1:SUBLANES], pltpu.roll(x0, 1, 0))
        xm2 = pltpu.roll(x0, 2, 0)
        xm2 = jnp.where(rows == 0, pv[SUBLANES - 2:SUBLANES - 1], xm2)
        xm2 = jnp.where(rows == 1, pv[SUBLANES - 1:SUBLANES], xm2)
        xp1 = jnp.where(rows == ts - 1, nv[0:1], pltpu.roll(x0, ts - 1, 0))
        xc = (xm2 * cw_ref[0:1] + xm1 * cw_ref[1:2] + x0 * cw_ref[2:3] + xp1 * cw_ref[3:4]) + cbias_ref[...]
        xconv_ref[0] = xc

    xcb = xc.astype(BF16)
    for j in range(cb // LRU_BW):
        sl = slice(j * LRU_BW, (j + 1) * LRU_BW)
        gts = jnp.dot(xcb[:, sl], wg_ref[j], preferred_element_type=F32)
        tr = jnp.tanh(gts[:, :LRU_BW] + br_ref[:, sl])
        ti = jnp.tanh(gts[:, LRU_BW:] + bi_ref[:, sl])
        nlam = -lam_ref[:, sl]
        softplus = jnp.maximum(nlam, 0.0) + jnp.log1p(jnp.exp(-jnp.abs(nlam)))
        c = (-0.5 * LRU_C) * softplus
        a = jnp.exp(c * tr + c)
        a_scr[:, sl] = a
        x = 1.0 - a * a
        half_root = 0.5 * jnp.where(x > 0.0, x * lax.rsqrt(x), 0.0)
        u_scr[:, sl] = half_root * ((ti + 1.0) * xc[:, sl])

    row8 = lax.broadcasted_iota(jnp.int32, (SUBLANES, cb), 0)
    nchunk = ts // SUBLANES

    def chunk(c, hprev):
        idx = (nchunk - 1 - c) if reverse else c
        r0 = pl.multiple_of(idx * SUBLANES, SUBLANES)
        a = a_scr[pl.ds(r0, SUBLANES), :]
        b = u_scr[pl.ds(r0, SUBLANES), :]
        for d in (1, 2, 4):
            if reverse:
                keep = row8 < SUBLANES - d
                sh = SUBLANES - d
            else:
                keep = row8 >= d
                sh = d
            a_s = jnp.where(keep, pltpu.roll(a, sh, 0), 1.0)
            b_s = jnp.where(keep, pltpu.roll(b, sh, 0), 0.0)
            b = a * b_s + b
            a = a * a_s
        h = a * hprev + b
        h_scr[pl.ds(r0, SUBLANES), :] = h
        edge = h[0:1] if reverse else h[SUBLANES - 1:SUBLANES]
        return jnp.broadcast_to(edge, (SUBLANES, cb))

    carry_scr[...] = lax.fori_loop(0, nchunk, chunk, carry_scr[...], unroll=4)

    if reverse:
        o_ref[0] = ((hf_ref[0] + h_scr[...]) * y_ref[0].astype(F32)).astype(BF16)


def _rglru(xr, wp, ts, cb, reverse, hf=None, y=None):
    B, S, C = xr.shape
    ns = S // ts
    nhb = S // SUBLANES
    hb = ts // SUBLANES
    d = 1 if reverse else 0
    tidx = (lambda i: ns - 1 - i) if reverse else (lambda i: i)
    cur = pl.BlockSpec((1, ts, cb), lambda b, c, i: (b, tidx(i), c))
    vec = pl.BlockSpec((1, cb), lambda b, c, i: (0, c))
    gate_specs = [pl.BlockSpec((cb // LRU_BW, LRU_BW, 2 * LRU_BW), lambda b, c, i: (c, 0, 0)), vec, vec, vec]
    gate_args = [wp["w_gate"][d], wp["b_r"][d], wp["b_i"][d], wp["lam"][d]]
    tile = pltpu.VMEM((ts, cb), F32)
    carry = pltpu.VMEM((SUBLANES, cb), F32)
    if reverse:
        in_specs = [cur] + gate_specs + [cur, cur]
        args = [xr] + gate_args + [hf, y]
        scratch = [tile, tile, tile, carry]
        out_specs, out_shape = cur, jax.ShapeDtypeStruct((B, S, C), BF16)
    else:
        prev = pl.BlockSpec((1, SUBLANES, cb), lambda b, c, i: (b, jnp.maximum(i * hb - 1, 0), c))
        nxt = pl.BlockSpec((1, SUBLANES, cb), lambda b, c, i: (b, jnp.minimum((i + 1) * hb, nhb - 1), c))
        in_specs = [prev, cur, nxt, pl.BlockSpec((CONV_W, cb), lambda b, c, i: (0, c)), vec] + gate_specs
        args = [xr, xr, xr, wp["conv_w"], wp["conv_b"]] + gate_args
        scratch = [tile, tile, carry]
        out_specs, out_shape = [cur, cur], [jax.ShapeDtypeStruct((B, S, C), F32)] * 2
    return pl.pallas_call(
        functools.partial(_rglru_kernel, ns=ns, ts=ts, cb=cb, reverse=reverse),
        grid=(B, C // cb, ns),
        in_specs=in_specs,
        out_specs=out_specs,
        out_shape=out_shape,
        scratch_shapes=scratch,
        compiler_params=_cparams(("parallel", "parallel", "arbitrary")),
    )(*args)


def _lane_table(vals, jidx, mask, fill):
    return jnp.where(mask[None, :], vals[:, jidx], fill)


def _rope_tables(S, half, period, start):
    inv = jnp.power(ROPE_THETA, -jnp.arange(half, dtype=F32) / half)
    ang = jnp.arange(S, dtype=F32)[:, None] * inv[None, :]
    cos, sin = jnp.cos(ang), jnp.sin(ang)
    e = np.arange(LANES) % period - start
    first = (e >= 0) & (e < half)
    second = (e >= half) & (e < 2 * half)
    jidx = np.where(first | second, e % half, 0)
    c = _lane_table(cos, jidx, first | second, 1.0)
    s1 = _lane_table(-sin, jidx, first, 0.0)
    s2 = _lane_table(sin, jidx, second, 0.0)
    return jnp.stack([c, s1, s2])


def _all_tables(S):
    return jnp.concatenate([
        _rope_tables(S, MLA_ROPE // 2, MLA_PAD, MLA_NOPE),
        _rope_tables(S, DIL_ROT // 2, HEAD_DIM, 0),
    ])


def _prep_weights(norm_mix, w_in_a, q_norm, w_uq, kv_norm, w_ukv, w_out_a, w_in_r, conv_w, conv_b,
                  lru_w_gate, lru_b_gate, lru_lambda, w_out_r, norm_ffn, w_gu, w_down, norm_final):
    w_in = w_in_a[0]
    qk_dim = MLA_NOPE + MLA_ROPE
    kr_cols = jnp.pad(w_in[:, MLA_Q_RANK + MLA_KV_RANK:MLA_IN],
                      ((0, 0), (MLA_NOPE, MLA_PAD - qk_dim)))
    w_mla = jnp.concatenate([w_in[:, :MLA_Q_RANK + MLA_KV_RANK], kr_cols], axis=1)
    uq = w_uq[0].reshape(MLA_Q_RANK, MLA_HEADS, qk_dim)
    uq = jnp.pad(uq, ((0, 0), (0, 0), (0, MLA_PAD - qk_dim))).reshape(MLA_Q_RANK, MLA_HEADS * MLA_PAD)
    ukv = w_ukv[0].reshape(MLA_KV_RANK, MLA_HEADS, MLA_NOPE + MLA_V)
    uk = jnp.pad(ukv[:, :, :MLA_NOPE], ((0, 0), (0, 0), (0, MLA_PAD - MLA_NOPE)))
    uk = uk.reshape(MLA_KV_RANK, MLA_HEADS * MLA_PAD)
    uv = ukv[:, :, MLA_NOPE:].reshape(MLA_KV_RANK, MLA_HEADS * MLA_V).T
    wg = lru_w_gate[0]
    w_gate = jnp.concatenate([wg[:, 0], wg[:, 1]], axis=-1)
    row = lambda v: v.reshape(1, -1)
    return {
        "norm_mix": [row(norm_mix[l]) for l in range(2)],
        "w_mla": w_mla.astype(BF16),
        "w_dil": w_in[:, MLA_IN:].astype(BF16),
        "q_norm": row(q_norm[0]),
        "w_uq": uq.astype(BF16),
        "kv_norm": row(kv_norm[0]),
        "w_uk": uk.astype(BF16),
        "w_uv": uv.astype(BF16),
        "w_out_a": w_out_a[0].astype(BF16),
        "w_in_r": w_in_r[0].astype(BF16),
        "conv_w": conv_w[0],
        "conv_b": row(conv_b[0]),
        "w_gate": (0.5 * w_gate).astype(BF16),
        "b_r": [row(0.5 * lru_b_gate[0, d, 0]) for d in range(2)],
        "b_i": [row(0.5 * lru_b_gate[0, d, 1]) for d in range(2)],
        "lam": [row(lru_lambda[0, d]) for d in range(2)],
        "w_out_r": w_out_r[0].astype(BF16),
        "norm_ffn": [row(norm_ffn[l]) for l in range(2)],
        "w_gu": [w_gu[l].astype(BF16) for l in range(2)],
        "w_down": [w_down[l].astype(BF16) for l in range(2)],
        "norm_final": row(norm_final),
    }


def _trunk(x, wp):
    B, S, D = x.shape
    T = B * S
    tabs = _all_tables(S)
    q, k, vt, *dgs = _l0_prep(x, wp["norm_mix"][0], wp, tabs, tm=512)
    tq, cq = min(S, 8192), 1024
    o_mla = _mla_attention(q, k, vt, tq=tq, tk=min(S, 2048), cq=cq, hps=min(4, max(2, 16 * cq // tq)))
    os_, ls_ = [], []
    for dg, (_, dil) in zip(dgs, DIL_PATTERNS):
        o_g, l_g = _dil_attention(dg, dil, tt=min(S // dil, 512))
        os_.append(o_g.reshape(T // dil, dil * DIL_W))
        ls_.append(l_g.reshape(T // dil, dil * DIL_W))
    x1 = _out_ffn_attn(x.reshape(T, D), o_mla.reshape(T, MLA_HEADS * MLA_V), os_, ls_, wp, 0, tm=512)
    y, xr = _l1_prep(x1, wp["norm_mix"][1], wp["w_in_r"], tm=512)
    xr = xr.reshape(B, S, D_RNN)
    hf, xconv = _rglru(xr, wp, ts=512, cb=D_RNN, reverse=False)
    a = _rglru(xconv, wp, ts=512, cb=D_RNN, reverse=True, hf=hf, y=y.reshape(B, S, D_RNN))
    out = _out_ffn_rec(x1, a.reshape(T, D_RNN), wp, 1, tm=512)
    return out.reshape(B, S, D)


def kernel(x_prompt, x_sample, norm_mix, w_in_a, q_norm, w_uq, kv_norm, w_ukv, w_out_a, w_in_r, conv_w,
           conv_b, lru_w_gate, lru_b_gate, lru_lambda, w_out_r, norm_ffn, w_gu, w_down, norm_final):
    wp = _prep_weights(norm_mix, w_in_a, q_norm, w_uq, kv_norm, w_ukv, w_out_a, w_in_r, conv_w, conv_b,
                       lru_w_gate, lru_b_gate, lru_lambda, w_out_r, norm_ffn, w_gu, w_down, norm_final)
    return (_trunk(x_prompt, wp), _trunk(x_sample, wp))
```
